```python
import math
import jax, jax.numpy as jnp
from jax import lax
import numpy as np

D_MODEL = 1024
BATCH = 4
SEQ = 8192
DEPTH = 1

RMS_EPS = 1e-6
ROPE_THETA = 10000.0
GLA_HEADS = 4
GLA_DK = D_MODEL // 2 // GLA_HEADS
GLA_DV = D_MODEL // GLA_HEADS
GLA_QK = GLA_HEADS * GLA_DK
GLA_V = GLA_HEADS * GLA_DV
GLA_LOWRANK = 16
GLA_GATE_NORM = 16.0
GLA_CHUNK = 64
DIFF_HEADS = 8
DIFF_HD = D_MODEL // (2 * DIFF_HEADS)
DIFF_VD = 2 * DIFF_HD
DIFF_QK = DIFF_HEADS * 2 * DIFF_HD
DIFF_V = DIFF_HEADS * DIFF_VD
Q_BLOCK = 128
PEER_HEADS = 8
PEER_NKEYS = 128
PEER_TOPK = 16
PEER_DQ = 256
PEER_NEXPERTS = PEER_NKEYS * PEER_NKEYS
PEER_BLOCK = 128
N_ADA = 6
POS_OFFSET_MAX = 1024

kernel_name = "hybrid_gla_diffattn_peer_adaln"


def rms_norm(x, g):
    xf = x.astype(jnp.float32)
    y = xf * lax.rsqrt(jnp.mean(xf * xf, axis=-1, keepdims=True) + RMS_EPS)
    return (y * g.astype(jnp.float32)).astype(x.dtype)


def modulate(h, shift, scale):
    return h * (1.0 + scale[:, None, :]) + shift[:, None, :]


def rope(x, pos):
    hd = x.shape[-1]
    inv = ROPE_THETA ** (-jnp.arange(0, hd, 2, dtype=jnp.float32) / hd)
    ang = pos.astype(jnp.float32)[..., None] * inv
    ang = jnp.concatenate([ang, ang], axis=-1)
    shape = ang.shape[:2] + (1,) * (x.ndim - 3) + (hd,)
    cos = jnp.cos(ang).reshape(shape)
    sin = jnp.sin(ang).reshape(shape)
    xf = x.astype(jnp.float32)
    x1, x2 = xf[..., : hd // 2], xf[..., hd // 2:]
    rot = jnp.concatenate([-x2, x1], axis=-1)
    return (xf * cos + rot * sin).astype(x.dtype)


def gla_scan(q, k, v, logg):
    B, H, S, dk = q.shape
    dv = v.shape[-1]
    nc = S // GLA_CHUNK
    f32 = jnp.float32
    qc = q.astype(f32).reshape(B, H, nc, GLA_CHUNK, dk)
    kc = k.astype(f32).reshape(B, H, nc, GLA_CHUNK, dk)
    vc = v.astype(f32).reshape(B, H, nc, GLA_CHUNK, dv)
    b = jnp.cumsum(logg.astype(f32).reshape(B, H, nc, GLA_CHUNK, dk), axis=3)
    b_last = b[..., -1:, :]
    qe = qc * jnp.exp(b)
    ke = kc * jnp.exp(-b)
    kend = kc * jnp.exp(b_last - b)
    mask = jnp.tril(jnp.ones((GLA_CHUNK, GLA_CHUNK), dtype=bool))
    att = jnp.where(mask, jnp.einsum('bhncd,bhnjd->bhncj', qe, ke), 0.0)
    o_intra = jnp.einsum('bhncj,bhnje->bhnce', att, vc)
    dec = jnp.exp(b_last[..., 0, :])

    def step(state, inp):
        qn, kn, vn, dn = inp
        o = jnp.einsum('bhcd,bhde->bhce', qn, state)
        state = dn[..., None] * state + jnp.einsum('bhcd,bhce->bhde', kn, vn)
        return state, o

    xs = (jnp.moveaxis(qe, 2, 0), jnp.moveaxis(kend, 2, 0),
          jnp.moveaxis(vc, 2, 0), jnp.moveaxis(dec, 2, 0))
    _, o_inter = lax.scan(step, jnp.zeros((B, H, dk, dv), f32), xs)
    o = o_intra + jnp.moveaxis(o_inter, 0, 2)
    return o.reshape(B, H, S, dv)


def diff_attention(q, k, v, lam, pos):
    B, S, H, _, hd = q.shape
    vd = v.shape[-1]
    q = rope(q, pos) * (hd ** -0.5)
    k = rope(k, pos)
    qh = q.transpose(0, 2, 3, 1, 4)
    kh = k.transpose(0, 2, 3, 1, 4)
    vh = v.transpose(0, 2, 1, 3)
    nb = S // Q_BLOCK
    qb = jnp.moveaxis(qh.reshape(B, H, 2, nb, Q_BLOCK, hd), 3, 0)

    def block(qblk):
        s = jnp.einsum('bhmqd,bhmkd->bhmqk', qblk, kh).astype(jnp.float32)
        p = jax.nn.softmax(s, axis=-1)
        a = p[:, :, 0] - lam * p[:, :, 1]
        return jnp.einsum('bhqk,bhkd->bhqd', a.astype(vh.dtype), vh)

    o = lax.map(block, qb)
    o = jnp.moveaxis(o, 0, 2).reshape(B, H, S, vd)
    return o.transpose(0, 2, 1, 3)


def peer(h, wq, subkeys, u_tab, v_tab):
    B, S, D = h.shape
    K = PEER_TOPK
    q = (h @ wq).reshape(B, S, PEER_HEADS, 2, PEER_DQ // 2)
    s = jnp.einsum('bshpd,hpnd->bshpn', q, subkeys).astype(jnp.float32)
    s_top, i_top = lax.top_k(s, K)
    cand_s = (s_top[..., 0, :, None] + s_top[..., 1, None, :]).reshape(B, S, PEER_HEADS, K * K)
    cand_i = (i_top[..., 0, :, None] * PEER_NKEYS + i_top[..., 1, None, :]).reshape(B, S, PEER_HEADS, K * K)
    s_fin, sel = lax.top_k(cand_s, K)
    idx = jnp.take_along_axis(cand_i, sel, axis=-1)
    g = jax.nn.softmax(s_fin, axis=-1)
    nb = (B * S) // PEER_BLOCK
    hb = h.reshape(nb, PEER_BLOCK, D)
    ib = idx.reshape(nb, PEER_BLOCK, PEER_HEADS * K)
    gb = g.reshape(nb, PEER_BLOCK, PEER_HEADS * K).astype(h.dtype)

    def block(args):
        hx, ix, gx = args
        u = u_tab[ix]
        act = jax.nn.gelu(jnp.einsum('td,ted->te', hx, u), approximate=False)
        vv = v_tab[ix]
        return jnp.einsum('te,ted->td', gx * act, vv)

    out = lax.map(block, (hb, ib, gb))
    return out.reshape(B, S, D)


def setup_inputs(seed: int = 0) -> dict:
    key = jax.random.key(seed)
    ks = jax.random.split(key, 32)
    D = D_MODEL
    in_width = 2 * GLA_QK + 2 * GLA_V + 2 * GLA_LOWRANK + 2 * DIFF_QK + DIFF_V + 2 * D

    def nrm(k, shape, scale):
        return jax.random.normal(k, shape, jnp.float32) * scale

    def gain(k, shape):
        return 1.0 + 0.02 * jax.random.normal(k, shape, jnp.float32)

    positions = (jnp.arange(SEQ, dtype=jnp.int32)[None, :]
                 + jax.random.randint(ks[2], (BATCH, 1), 0, POS_OFFSET_MAX, dtype=jnp.int32))
    return {
        "x": nrm(ks[0], (BATCH, SEQ, D), 1.0),
        "c": nrm(ks[1], (BATCH, D), 1.0),
        "positions": positions,
        "w_ada": nrm(ks[3], (DEPTH, D, N_ADA * D), 0.5 * D ** -0.5),
        "b_ada": nrm(ks[4], (DEPTH, N_ADA * D), 0.01),
        "norm1_g": gain(ks[5], (DEPTH, D)),
        "w_in": nrm(ks[6], (DEPTH, D, in_width), D ** -0.5),
        "gla_wa_fw": nrm(ks[7], (DEPTH, GLA_LOWRANK, GLA_QK), GLA_LOWRANK ** -0.5),
        "gla_ba_fw": nrm(ks[8], (DEPTH, GLA_QK), 0.01),
        "gla_wa_bw": nrm(ks[9], (DEPTH, GLA_LOWRANK, GLA_QK), GLA_LOWRANK ** -0.5),
        "gla_ba_bw": nrm(ks[10], (DEPTH, GLA_QK), 0.01),
        "gla_norm_g": gain(ks[11], (DEPTH, GLA_V)),
        "diff_lq1": nrm(ks[12], (DEPTH, DIFF_HD), 0.1),
        "diff_lk1": nrm(ks[13], (DEPTH, DIFF_HD), 0.1),
        "diff_lq2": nrm(ks[14], (DEPTH, DIFF_HD), 0.1),
        "diff_lk2": nrm(ks[15], (DEPTH, DIFF_HD), 0.1),
        "diff_norm_g": gain(ks[16], (DEPTH, DIFF_V)),
        "w_gla_proj": nrm(ks[17], (DEPTH, GLA_V, D), GLA_V ** -0.5),
        "w_diff_proj": nrm(ks[18], (DEPTH, DIFF_V, D), DIFF_V ** -0.5),
        "w_out": nrm(ks[19], (DEPTH, D, D), D ** -0.5),
        "norm2_g": gain(ks[20], (DEPTH, D)),
        "peer_wq": nrm(ks[21], (DEPTH, D, PEER_HEADS * PEER_DQ), D ** -0.5),
        "peer_subkeys": nrm(ks[22], (DEPTH, PEER_HEADS, 2, PEER_NKEYS, PEER_DQ // 2), (PEER_DQ // 2) ** -0.5),
        "peer_u": nrm(ks[23], (DEPTH, PEER_NEXPERTS, D), D ** -0.5),
        "peer_v": nrm(ks[24], (DEPTH, PEER_NEXPERTS, D), PEER_HEADS ** -0.5),
        "w_final_ada": nrm(ks[25], (D, 2 * D), 0.5 * D ** -0.5),
        "b_final_ada": nrm(ks[26], (2 * D,), 0.01),
        "normf_g": gain(ks[27], (D,)),
    }


def reference(x, c, positions, w_ada, b_ada, norm1_g, w_in, gla_wa_fw, gla_ba_fw,
              gla_wa_bw, gla_ba_bw, gla_norm_g, diff_lq1, diff_lk1, diff_lq2, diff_lk2,
              diff_norm_g, w_gla_proj, w_diff_proj, w_out, norm2_g, peer_wq,
              peer_subkeys, peer_u, peer_v, w_final_ada, b_final_ada, normf_g):
    B, S, D = x.shape
    splits = [GLA_QK, GLA_QK, GLA_V, GLA_V, GLA_LOWRANK, GLA_LOWRANK,
              DIFF_QK, DIFF_QK, DIFF_V, D, D]
    offsets = np.cumsum(splits)[:-1].tolist()
    c_act = jax.nn.silu(c)

    for l in range(DEPTH):
        lambda_init = 0.8 - 0.6 * math.exp(-0.3 * l)
        mod = c_act @ w_ada[l] + b_ada[l]
        sh1, sc1, gt1, sh2, sc2, gt2 = jnp.split(mod, N_ADA, axis=-1)

        h = modulate(rms_norm(x, norm1_g[l]), sh1, sc1)
        proj = h @ w_in[l]
        (gq, gk, gv, gr, lr_fw, lr_bw, dq, dk, dv, gate_a, gate_b) = jnp.split(proj, offsets, axis=-1)

        def heads(t, d):
            return t.reshape(B, S, GLA_HEADS, d).transpose(0, 2, 1, 3)
        q_g = heads(gq, GLA_DK) * (GLA_DK ** -0.5)
        k_g = heads(gk, GLA_DK)
        v_g = heads(gv, GLA_DV)
        logg_fw = heads(jax.nn.log_sigmoid((lr_fw @ gla_wa_fw[l] + gla_ba_fw[l]).astype(jnp.float32)) / GLA_GATE_NORM, GLA_DK)
        logg_bw = heads(jax.nn.log_sigmoid((lr_bw @ gla_wa_bw[l] + gla_ba_bw[l]).astype(jnp.float32)) / GLA_GATE_NORM, GLA_DK)
        o_fw = gla_scan(q_g, k_g, v_g, logg_fw)
        o_bw = jnp.flip(gla_scan(jnp.flip(q_g, 2), jnp.flip(k_g, 2), jnp.flip(v_g, 2),
                                 jnp.flip(logg_bw, 2)), 2)
        o_g = (o_fw + o_bw).transpose(0, 2, 1, 3).astype(x.dtype)
        o_g = rms_norm(o_g, gla_norm_g[l].reshape(GLA_HEADS, GLA_DV)).reshape(B, S, GLA_V)
        y_gla = (o_g * jax.nn.silu(gr)) @ w_gla_proj[l]

        lam = (jnp.exp(jnp.sum(diff_lq1[l].astype(jnp.float32) * diff_lk1[l].astype(jnp.float32)))
               - jnp.exp(jnp.sum(diff_lq2[l].astype(jnp.float32) * diff_lk2[l].astype(jnp.float32)))
               + lambda_init)
        q_d = dq.reshape(B, S, DIFF_HEADS, 2, DIFF_HD)
        k_d = dk.reshape(B, S, DIFF_HEADS, 2, DIFF_HD)
        v_d = dv.reshape(B, S, DIFF_HEADS, DIFF_VD)
        o_d = diff_attention(q_d, k_d, v_d, lam, positions).astype(x.dtype)
        o_d = rms_norm(o_d, diff_norm_g[l].reshape(DIFF_HEADS, DIFF_VD)) * (1.0 - lambda_init)
        y_diff = o_d.reshape(B, S, DIFF_V) @ w_diff_proj[l]

        merged = jax.nn.sigmoid(gate_a) * y_gla + jax.nn.sigmoid(gate_b) * y_diff
        x = x + gt1[:, None, :] * (merged @ w_out[l])

        h2 = modulate(rms_norm(x, norm2_g[l]), sh2, sc2)
        x = x + gt2[:, None, :] * peer(h2, peer_wq[l], peer_subkeys[l], peer_u[l], peer_v[l])

    fmod = c_act @ w_final_ada + b_final_ada
    f_shift, f_scale = jnp.split(fmod, 2, axis=-1)
    return modulate(rms_norm(x, normf_g), f_shift, f_scale)
```

```python
import functools
import math

import numpy as np
import jax
import jax.numpy as jnp
from jax import lax
from jax.experimental import pallas as pl
from jax.experimental.pallas import tpu as pltpu

F32 = jnp.float32
BF16 = jnp.bfloat16

RMS_EPS = 1e-6
ROPE_THETA = 10000.0
GLA_HEADS = 4
GLA_DK = 128
GLA_DV = 256
GLA_LOWRANK = 16
GLA_GATE_NORM = 16.0
GLA_CHUNK = 64
DIFF_HEADS = 8
DIFF_HD = 64
DIFF_VD = 128
PEER_HEADS = 8
PEER_NKEYS = 128
PEER_TOPK = 16
N_ADA = 6

LANES = 128
VMEM_LIMIT = 48 * 1024 * 1024

NT_DIMS = (((1,), (1,)), ((), ()))


def _cparams(*sem):
    return pltpu.CompilerParams(dimension_semantics=sem, vmem_limit_bytes=VMEM_LIMIT)


def _dot(a, b):
    return jnp.dot(a, b, preferred_element_type=F32)


def _dot_nt(a, b):
    return lax.dot_general(a, b, NT_DIMS, preferred_element_type=F32)


def _rms(x, g):
    ms = jnp.mean(x * x, axis=-1, keepdims=True)
    return x * lax.rsqrt(ms + RMS_EPS) * g


def _resident(shape):
    nd = len(shape)
    return pl.BlockSpec(shape, lambda *_: (0,) * nd)


def _ada_kernel(c_ref, w_ref, b_ref, o_ref):
    c = c_ref[...]
    ca = c * jax.nn.sigmoid(c)
    o_ref[...] = jnp.dot(ca, w_ref[...], preferred_element_type=F32,
                         precision=lax.Precision.HIGHEST) + b_ref[...]


def _ada(c, w, b):
    bsz, d = c.shape
    n = w.shape[1]
    tn = 2048
    return pl.pallas_call(
        _ada_kernel,
        grid=(n // tn,),
        in_specs=[pl.BlockSpec((bsz, d), lambda j: (0, 0)),
                  pl.BlockSpec((d, tn), lambda j: (0, j)),
                  pl.BlockSpec((1, tn), lambda j: (0, j))],
        out_specs=pl.BlockSpec((bsz, tn), lambda j: (0, j)),
        out_shape=jax.ShapeDtypeStruct((bsz, n), F32),
        compiler_params=_cparams("arbitrary"),
        name="ada",
    )(c, w, b.reshape(1, n))


_C_GQ, _C_GK, _C_GV, _C_GR, _C_DK, _C_GA, _C_GB, _C_END = 0, 512, 1024, 2048, 3072, 4096, 5120, 6144
_R_DQ, _R_DV, _R_GV, _R_END = 0, 1024, 2048, 3072


def _inproj_kernel(x_ref, sh_ref, sc_ref, g_ref, pos_ref, post_ref, invr_ref, invc_ref,
                   wn_ref, wlr_ref, wt_ref,
                   gq_ref, gk_ref, gv_ref, gr_ref, lr_ref, dk_ref, ga_ref, gb_ref,
                   dqt_ref, dvt_ref, gvt_ref):
    tm = x_ref.shape[1]
    h = _rms(x_ref[0], g_ref[...]) * (1.0 + sc_ref[0]) + sh_ref[0]
    hb = h.astype(BF16)

    def proj(c0, c1):
        return _dot(hb, wn_ref[:, c0:c1])

    gq_ref[0] = (proj(_C_GQ, _C_GK) * (GLA_DK ** -0.5)).astype(BF16)
    gk_ref[0] = proj(_C_GK, _C_GV).astype(BF16)
    gv_ref[0] = proj(_C_GV, _C_GR).astype(BF16)
    gr_ref[0] = proj(_C_GR, _C_DK).astype(BF16)
    ga_ref[0] = proj(_C_GA, _C_GB).astype(BF16)
    gb_ref[0] = proj(_C_GB, _C_END).astype(BF16)
    lr_ref[0] = _dot(hb, wlr_ref[...])

    ang = pos_ref[0] * invr_ref[...]
    cs = jnp.cos(ang)
    sn = jnp.sin(ang)
    lane = lax.broadcasted_iota(jnp.int32, (tm, LANES), 1)
    first = (lane % DIFF_HD) < (DIFF_HD // 2)
    sn = jnp.where(first, -sn, sn)
    y = proj(_C_DK, _C_GA)
    for hd in range(DIFF_HEADS):
        yb = y[:, hd * LANES:(hd + 1) * LANES]
        rot = jnp.where(first, pltpu.roll(yb, LANES - DIFF_HD // 2, 1), pltpu.roll(yb, DIFF_HD // 2, 1))
        dk_ref[0, :, hd * LANES:(hd + 1) * LANES] = (yb * cs + rot * sn).astype(BF16)

    angt = invc_ref[...] * post_ref[0]
    ct = jnp.cos(angt)
    st = jnp.sin(angt)
    half = DIFF_HD // 2
    yt = _dot_nt(wt_ref[_R_DQ:_R_DV, :], hb)
    qscale = DIFF_HD ** -0.5
    for blk in range(2 * DIFF_HEADS):
        r0 = blk * DIFF_HD
        x1 = yt[r0:r0 + half]
        x2 = yt[r0 + half:r0 + DIFF_HD]
        dqt_ref[0, r0:r0 + half, :] = ((x1 * ct - x2 * st) * qscale).astype(BF16)
        dqt_ref[0, r0 + half:r0 + DIFF_HD, :] = ((x2 * ct + x1 * st) * qscale).astype(BF16)
    dvt_ref[0] = _dot_nt(wt_ref[_R_DV:_R_GV, :], hb).astype(BF16)
    gvt_ref[0] = _dot_nt(wt_ref[_R_GV:_R_END, :], hb).astype(BF16)


def _inproj(x, sh, sc, g, pos, wn, wlr, wt, inv_row, inv_col, tm):
    bsz, s, d = x.shape
    posf = pos.astype(F32)
    nat = lambda n: pl.BlockSpec((1, tm, n), lambda b, i: (b, i, 0))
    tr = lambda n: pl.BlockSpec((1, n, tm), lambda b, i: (b, 0, i))
    vec = pl.BlockSpec((1, 1, d), lambda b, i: (b, 0, 0))
    shp = lambda *dims: jax.ShapeDtypeStruct(dims, BF16)
    return pl.pallas_call(
        _inproj_kernel,
        grid=(bsz, s // tm),
        in_specs=[nat(d), vec, vec, _resident((1, d)),
                  pl.BlockSpec((1, tm, 1), lambda b, i: (b, i, 0)),
                  pl.BlockSpec((1, 1, tm), lambda b, i: (b, 0, i)),
                  _resident(inv_row.shape), _resident(inv_col.shape),
                  _resident(wn.shape), _resident(wlr.shape), _resident(wt.shape)],
        out_specs=[nat(512), nat(512), nat(1024), nat(1024), nat(LANES), nat(1024), nat(1024), nat(1024),
                   tr(1024), tr(1024), tr(1024)],
        out_shape=[shp(bsz, s, 512), shp(bsz, s, 512), shp(bsz, s, 1024), shp(bsz, s, 1024),
                   jax.ShapeDtypeStruct((bsz, s, LANES), F32),
                   shp(bsz, s, 1024), shp(bsz, s, 1024), shp(bsz, s, 1024),
                   shp(bsz, 1024, s), shp(bsz, 1024, s), shp(bsz, 1024, s)],
        compiler_params=_cparams("arbitrary", "arbitrary"),
        name="inproj",
    )(x, sh, sc, g, posf.reshape(bsz, s, 1), posf.reshape(bsz, 1, s), inv_row, inv_col, wn, wlr, wt)


def _gla_direction(q_ref, k_ref, v_ref, vt_ref, lr_ref, wh_ref, wl_ref, ba_ref, tri_ref, o_ref, s_ref, reverse):
    blk = q_ref.shape[1]
    lr = lr_ref[0]
    lr_hi = lr.astype(BF16)
    lr_lo = (lr - lr_hi.astype(F32)).astype(BF16)
    z = _dot(lr_hi, wh_ref[...]) + _dot(lr_lo, wh_ref[...]) + _dot(lr_hi, wl_ref[...]) + ba_ref[...]
    logg = -(jnp.maximum(-z, 0.0) + jnp.log1p(jnp.exp(-jnp.abs(z)))) * (1.0 / GLA_GATE_NORM)
    lg_hi = logg.astype(BF16)
    lg_lo = (logg - lg_hi.astype(F32)).astype(BF16)
    tri = tri_ref[...]
    b = _dot(tri, lg_hi) + _dot(tri, lg_lo)
    q = q_ref[0].astype(F32)
    k = k_ref[0].astype(F32)
    v = v_ref[0]
    qe = (q * jnp.exp(b)).astype(BF16)
    ke = (k * jnp.exp(-b)).astype(BF16)
    att = jnp.where(tri > 0, _dot_nt(qe, ke), 0.0).astype(BF16)
    o_intra = _dot(att, v)

    state = s_ref[...]
    nchunk = blk // GLA_CHUNK
    order = range(nchunk - 1, -1, -1) if reverse else range(nchunk)
    for n in order:
        r0 = n * GLA_CHUNK
        edge = r0 if reverse else r0 + GLA_CHUNK - 1
        b_c = b[r0:r0 + GLA_CHUNK]
        b_e = b[edge:edge + 1]
        kend = (k[r0:r0 + GLA_CHUNK] * jnp.exp(b_e - b_c)).astype(BF16)
        o_inter = _dot_nt(qe[r0:r0 + GLA_CHUNK], state.astype(BF16))
        o_ref[0, r0:r0 + GLA_CHUNK, :] = (o_intra[r0:r0 + GLA_CHUNK] + o_inter).astype(BF16)
        state = state * jnp.exp(b_e) + _dot(vt_ref[0, :, r0:r0 + GLA_CHUNK], kend)
    s_ref[...] = state


def _gla_kernel(qf, kf, vf, vtf, lrf, qb, kb, vb, vtb, lrb,
                wfh, wfl, baf, wbh, wbl, bab, trif, trib,
                of_ref, ob_ref, sf_ref, sb_ref):
    @pl.when(pl.program_id(2) == 0)
    def _():
        sf_ref[...] = jnp.zeros_like(sf_ref)
        sb_ref[...] = jnp.zeros_like(sb_ref)

    _gla_direction(qf, kf, vf, vtf, lrf, wfh, wfl, baf, trif, of_ref, sf_ref, reverse=False)
    _gla_direction(qb, kb, vb, vtb, lrb, wbh, wbl, bab, trib, ob_ref, sb_ref, reverse=True)


def _gla(gq, gk, gv, gvt, lr, wa_f, ba_f, wa_b, ba_b, blk):
    bsz, s, _ = gq.shape
    nblk = s // blk
    def pad_rows(w, r0):
        return jnp.zeros((LANES, w.shape[1]), F32).at[r0:r0 + GLA_LOWRANK].set(w)

    def hi_lo(w):
        hi = w.astype(BF16)
        return hi, (w - hi.astype(F32)).astype(BF16)

    wfh, wfl = hi_lo(pad_rows(wa_f, 0))
    wbh, wbl = hi_lo(pad_rows(wa_b, GLA_LOWRANK))
    r = np.arange(blk)
    same = (r[:, None] // GLA_CHUNK) == (r[None, :] // GLA_CHUNK)
    tri_f = jnp.asarray(same & (r[None, :] <= r[:, None]), BF16)
    tri_b = jnp.asarray(same & (r[None, :] >= r[:, None]), BF16)

    fw = lambda b, h, i: (b, i, h)
    bw = lambda b, h, i: (b, nblk - 1 - i, h)
    fwt = lambda b, h, i: (b, h, i)
    bwt = lambda b, h, i: (b, h, nblk - 1 - i)

    def seq_specs(tok, tok_t):
        return [pl.BlockSpec((1, blk, GLA_DK), tok), pl.BlockSpec((1, blk, GLA_DK), tok),
                pl.BlockSpec((1, blk, GLA_DV), tok), pl.BlockSpec((1, GLA_DV, blk), tok_t),
                pl.BlockSpec((1, blk, LANES), lambda b, h, i, t=tok: (t(b, h, i)[0], t(b, h, i)[1], 0))]

    wspec = pl.BlockSpec((LANES, GLA_DK), lambda b, h, i: (0, h))
    bspec = pl.BlockSpec((1, GLA_DK), lambda b, h, i: (0, h))
    out_f = pl.BlockSpec((1, blk, GLA_DV), fw)
    out_b = pl.BlockSpec((1, blk, GLA_DV), bw)
    o_shape = jax.ShapeDtypeStruct((bsz, s, GLA_HEADS * GLA_DV), BF16)
    return pl.pallas_call(
        _gla_kernel,
        grid=(bsz, GLA_HEADS, nblk),
        in_specs=seq_specs(fw, fwt) + seq_specs(bw, bwt) + [wspec, wspec, bspec, wspec, wspec, bspec,
                                                             _resident((blk, blk)), _resident((blk, blk))],
        out_specs=[out_f, out_b],
        out_shape=[o_shape, o_shape],
        scratch_shapes=[pltpu.VMEM((GLA_DV, GLA_DK), F32), pltpu.VMEM((GLA_DV, GLA_DK), F32)],
        compiler_params=_cparams("arbitrary", "arbitrary", "arbitrary"),
        name="gla",
    )(gq, gk, gv, gvt, lr, gq, gk, gv, gvt, lr,
      wfh, wfl, ba_f.reshape(1, -1), wbh, wbl, ba_b.reshape(1, -1), tri_f, tri_b)


def _attn_kernel(lq1_ref, lk1_ref, lq2_ref, lk2_ref, qt_ref, k_ref, vt_ref, o_ref, m_ref, l_ref, acc_ref,
                 *, tk, lambda_init):
    tq = qt_ref.shape[2]
    nk = k_ref.shape[1] // tk
    qt = qt_ref[0]
    row = lax.broadcasted_iota(jnp.int32, qt.shape, 0)
    zero = jnp.zeros_like(qt)
    qmaps = (jnp.where(row < DIFF_HD, qt, zero), jnp.where(row >= DIFF_HD, qt, zero))
    m_ref[...] = jnp.full(m_ref.shape, -jnp.inf, F32)
    l_ref[...] = jnp.zeros_like(l_ref)
    acc_ref[...] = jnp.zeros_like(acc_ref)

    def body(j, carry):
        k0 = pl.multiple_of(j * tk, tk)
        kj = k_ref[0, pl.ds(k0, tk), :]
        vtj = vt_ref[0, :, pl.ds(k0, tk)]
        for mi in range(2):
            s = _dot(kj, qmaps[mi])
            m_old = m_ref[mi]
            m_new = jnp.maximum(m_old, jnp.max(s, axis=0, keepdims=True))
            alpha = jnp.exp(m_old - m_new)
            p = jnp.exp(s - m_new)
            l_ref[mi] = alpha * l_ref[mi] + jnp.sum(p, axis=0, keepdims=True)
            acc_ref[mi] = acc_ref[mi] * alpha + _dot(vtj, p.astype(BF16))
            m_ref[mi] = m_new
        return carry

    lax.fori_loop(0, nk, body, 0)
    lam = (jnp.exp(jnp.sum(lq1_ref[...] * lk1_ref[...], axis=-1, keepdims=True))
           - jnp.exp(jnp.sum(lq2_ref[...] * lk2_ref[...], axis=-1, keepdims=True)) + lambda_init)
    ot = acc_ref[0] * (1.0 / l_ref[0]) - lam * (acc_ref[1] * (1.0 / l_ref[1]))
    o_ref[0] = ot.T.astype(BF16)


def _attn(dqt, dk, dvt, lq1, lk1, lq2, lk2, lambda_init, tq, tk):
    bsz, s, _ = dk.shape
    lspec = _resident((1, DIFF_HD))
    return pl.pallas_call(
        functools.partial(_attn_kernel, tk=tk, lambda_init=lambda_init),
        grid=(bsz, DIFF_HEADS, s // tq),
        in_specs=[lspec, lspec, lspec, lspec,
                  pl.BlockSpec((1, 2 * DIFF_HD, tq), lambda b, h, i: (b, h, i)),
                  pl.BlockSpec((1, s, 2 * DIFF_HD), lambda b, h, i: (b, 0, h)),
                  pl.BlockSpec((1, DIFF_VD, s), lambda b, h, i: (b, h, 0))],
        out_specs=pl.BlockSpec((1, tq, DIFF_VD), lambda b, h, i: (b, i, h)),
        out_shape=jax.ShapeDtypeStruct((bsz, s, DIFF_HEADS * DIFF_VD), BF16),
        scratch_shapes=[pltpu.VMEM((2, 1, tq), F32), pltpu.VMEM((2, 1, tq), F32),
                        pltpu.VMEM((2, DIFF_VD, tq), F32)],
        compiler_params=_cparams("arbitrary", "arbitrary", "arbitrary"),
        name="attn",
    )(lq1.reshape(1, -1), lk1.reshape(1, -1), lq2.reshape(1, -1), lk2.reshape(1, -1), dqt, dk, dvt)


def _merge_kernel(of_ref, ob_ref, gr_ref, od_ref, ga_ref, gb_ref, x_ref, gt_ref, gg_ref, dg_ref,
                  wg_ref, wd_ref, wo_ref, o_ref, *, lambda_init):
    og = of_ref[0].astype(F32) + ob_ref[0].astype(F32)
    gg = gg_ref[...]
    og = jnp.concatenate(
        [_rms(og[:, h * GLA_DV:(h + 1) * GLA_DV], gg[:, h * GLA_DV:(h + 1) * GLA_DV]) for h in range(GLA_HEADS)],
        axis=-1)
    gr = gr_ref[0].astype(F32)
    y_gla = _dot((og * (gr * jax.nn.sigmoid(gr))).astype(BF16), wg_ref[...])
    od = od_ref[0].astype(F32)
    dg = dg_ref[...]
    od = jnp.concatenate(
        [_rms(od[:, h * DIFF_VD:(h + 1) * DIFF_VD], dg[:, h * DIFF_VD:(h + 1) * DIFF_VD]) for h in range(DIFF_HEADS)],
        axis=-1) * (1.0 - lambda_init)
    y_diff = _dot(od.astype(BF16), wd_ref[...])
    merged = (jax.nn.sigmoid(ga_ref[0].astype(F32)) * y_gla + jax.nn.sigmoid(gb_ref[0].astype(F32)) * y_diff)
    o_ref[0] = x_ref[0] + gt_ref[0] * _dot(merged.astype(BF16), wo_ref[...])


def _merge(o_f, o_b, gr, o_d, ga, gb, x, gt1, gla_g, diff_g, wg, wd, wo, lambda_init, tm):
    bsz, s, d = x.shape
    tile = pl.BlockSpec((1, tm, d), lambda b, i: (b, i, 0))
    vec = pl.BlockSpec((1, 1, d), lambda b, i: (b, 0, 0))
    return pl.pallas_call(
        functools.partial(_merge_kernel, lambda_init=lambda_init),
        grid=(bsz, s // tm),
        in_specs=[tile] * 7 + [vec, _resident((1, d)), _resident((1, d)),
                               _resident((d, d)), _resident((d, d)), _resident((d, d))],
        out_specs=tile,
        out_shape=jax.ShapeDtypeStruct((bsz, s, d), F32),
        compiler_params=_cparams("arbitrary", "arbitrary"),
        name="merge",
    )(o_f, o_b, gr, o_d, ga, gb, x, gt1, gla_g.reshape(1, d), diff_g.reshape(1, d), wg, wd, wo)


_CAND_ROW_COUNT = (16, 8, 5, 4, 3, 2, 2, 2)
_SUB = 8


def _top16(a):
    work = a
    rank = jnp.full(a.shape, float(PEER_TOPK), F32)
    vals = []
    for r in range(PEER_TOPK):
        m = jnp.max(work, axis=0, keepdims=True)
        vals.append(m)
        hit = work == m
        rank = jnp.where(hit, float(r), rank)
        work = jnp.where(hit, -jnp.inf, work)
    return jnp.concatenate(vals, axis=0), rank


def _route_head(a, b):
    t = a.shape[1]
    top_a, rank_a = _top16(a)
    top_b, rank_b = _top16(b)
    rowi = lax.broadcasted_iota(jnp.int32, (_SUB, t), 0)
    pieces = [top_b + top_a[0:1]]
    for r in range(1, _SUB):
        piece = top_b[0:_SUB] + top_a[r:r + 1]
        n = _CAND_ROW_COUNT[r]
        pieces.append(piece if n == _SUB else jnp.where(rowi < n, piece, -jnp.inf))
    pieces.append(top_a[_SUB:PEER_TOPK] + top_b[0:1])
    cand = jnp.concatenate(pieces, axis=0)
    work = cand
    thr = None
    for _ in range(PEER_TOPK):
        thr = jnp.max(work, axis=0, keepdims=True)
        work = jnp.where(work == thr, -jnp.inf, work)
    sel = cand >= thr
    smax = top_a[0:1] + top_b[0:1]
    z = jnp.sum(jnp.where(sel, jnp.exp(cand - smax), 0.0), axis=0, keepdims=True)
    self32 = sel.astype(F32)
    counts = [jnp.sum(self32[0:PEER_TOPK], axis=0, keepdims=True)]
    for r in range(1, _SUB):
        r0 = PEER_TOPK + (r - 1) * _SUB
        counts.append(jnp.sum(self32[r0:r0 + _SUB], axis=0, keepdims=True))
    tail0 = PEER_TOPK + (_SUB - 1) * _SUB
    for r in range(_SUB, PEER_TOPK):
        counts.append(self32[tail0 + r - _SUB:tail0 + r - _SUB + 1])
    cnt = jnp.zeros(a.shape, F32)
    for r in range(PEER_TOPK):
        cnt = jnp.where(rank_a == float(r), counts[r], cnt)
    wa = jnp.exp(a - top_a[0:1]) * (1.0 / z)
    wb = jnp.exp(b - top_b[0:1])
    return rank_b, wb, cnt, wa


def _route_kernel(x_ref, sh_ref, sc_ref, g_ref, wqt_ref, sk_ref, h2t_ref, rb_ref, wb_ref, cnt_ref, wa_ref):
    h2 = _rms(x_ref[0], g_ref[...]) * (1.0 + sc_ref[0]) + sh_ref[0]
    hb = h2.astype(BF16)
    h2t_ref[0] = h2.T.astype(BF16)
    qt = _dot_nt(wqt_ref[...], hb).astype(BF16)
    for h in range(PEER_HEADS):
        g0 = 2 * h * PEER_NKEYS
        a = _dot(sk_ref[2 * h], qt[g0:g0 + PEER_NKEYS])
        b = _dot(sk_ref[2 * h + 1], qt[g0 + PEER_NKEYS:g0 + 2 * PEER_NKEYS])
        rank_b, wb, cnt, wa = _route_head(a, b)
        rb_ref[0, h] = rank_b.astype(BF16)
        wb_ref[0, h] = wb.astype(BF16)
        cnt_ref[0, h] = cnt
        wa_ref[0, h] = wa


def _route(x1, sh2, sc2, g, wqt, sk, tm):
    bsz, s, d = x1.shape
    vec = pl.BlockSpec((1, 1, d), lambda b, i: (b, 0, 0))
    gate = pl.BlockSpec((1, PEER_HEADS, PEER_NKEYS, tm), lambda b, i: (b, 0, 0, i))
    gshape = lambda dt: jax.ShapeDtypeStruct((bsz, PEER_HEADS, PEER_NKEYS, s), dt)
    return pl.pallas_call(
        _route_kernel,
        grid=(bsz, s // tm),
        in_specs=[pl.BlockSpec((1, tm, d), lambda b, i: (b, i, 0)), vec, vec, _resident((1, d)),
                  _resident(wqt.shape), _resident(sk.shape)],
        out_specs=[pl.BlockSpec((1, d, tm), lambda b, i: (b, 0, i)), gate, gate, gate, gate],
        out_shape=[jax.ShapeDtypeStruct((bsz, d, s), BF16), gshape(BF16), gshape(BF16), gshape(F32), gshape(F32)],
        compiler_params=_cparams("arbitrary", "arbitrary"),
        name="route",
    )(x1, sh2, sc2, g, wqt, sk)


def _dense_kernel(h2t_ref, u_ref, vt_ref, rb_ref, wb_ref, cnt_ref, wa_ref, x_ref, gt_ref, fg_ref, fsh_ref, fsc_ref,
                  o_ref, acc_ref, *, eblk):
    e = pl.program_id(2)

    @pl.when(e == 0)
    def _():
        acc_ref[...] = jnp.zeros_like(acc_ref)

    at = _dot(u_ref[...], h2t_ref[0])
    act = 0.5 * at * (1.0 + lax.erf(at * math.sqrt(0.5)))
    gates = []
    for ii in range(eblk // PEER_NKEYS):
        i = e * (eblk // PEER_NKEYS) + ii
        w = None
        for h in range(PEER_HEADS):
            cnt_i = cnt_ref[0, h, pl.ds(i, 1), :].astype(BF16)
            wa_i = wa_ref[0, h, pl.ds(i, 1), :].astype(BF16)
            wh = jnp.where(rb_ref[0, h] < cnt_i, wb_ref[0, h], jnp.zeros((), BF16)) * wa_i
            w = wh if w is None else w + wh
        gates.append(w)
    gate = jnp.concatenate(gates, axis=0)
    p = (act * gate.astype(F32)).astype(BF16)
    acc_ref[...] += _dot(vt_ref[...], p)

    @pl.when(e == pl.num_programs(2) - 1)
    def _():
        y = x_ref[0] + gt_ref[0] * acc_ref[...].T
        o_ref[0] = _rms(y, fg_ref[...]) * (1.0 + fsc_ref[0]) + fsh_ref[0]


def _dense(h2t, u, vt, rb, wb, cnt, wa, x1, gt2, fg, fsh, fsc, tm, eblk):
    bsz, s, d = x1.shape
    ne = u.shape[0]
    gate = pl.BlockSpec((1, PEER_HEADS, PEER_NKEYS, tm), lambda b, i, e: (b, 0, 0, i))
    tile = pl.BlockSpec((1, tm, d), lambda b, i, e: (b, i, 0))
    vec = pl.BlockSpec((1, 1, d), lambda b, i, e: (b, 0, 0))
    return pl.pallas_call(
        functools.partial(_dense_kernel, eblk=eblk),
        grid=(bsz, s // tm, ne // eblk),
        in_specs=[pl.BlockSpec((1, d, tm), lambda b, i, e: (b, 0, i)),
                  pl.BlockSpec((eblk, d), lambda b, i, e: (e, 0)),
                  pl.BlockSpec((d, eblk), lambda b, i, e: (0, e)),
                  gate, gate, gate, gate, tile, vec, _resident((1, d)), vec, vec],
        out_specs=tile,
        out_shape=jax.ShapeDtypeStruct((bsz, s, d), F32),
        scratch_shapes=[pltpu.VMEM((d, tm), F32)],
        compiler_params=_cparams("arbitrary", "arbitrary", "arbitrary"),
        name="dense",
    )(h2t, u, vt, rb, wb, cnt, wa, x1, gt2, fg.reshape(1, d), fsh, fsc)


def _tile(s, want):
    return want if s % want == 0 else s


def kernel(x, c, positions, w_ada, b_ada, norm1_g, w_in, gla_wa_fw, gla_ba_fw, gla_wa_bw, gla_ba_bw, gla_norm_g,
           diff_lq1, diff_lk1, diff_lq2, diff_lk2, diff_norm_g, w_gla_proj, w_diff_proj, w_out, norm2_g, peer_wq,
           peer_subkeys, peer_u, peer_v, w_final_ada, b_final_ada, normf_g):
    bsz, s, d = x.shape
    depth = w_ada.shape[0]
    tm = _tile(s, 512)

    half = DIFF_HD // 2
    inv = (ROPE_THETA ** (-np.arange(0, DIFF_HD, 2, dtype=np.float32) / DIFF_HD)).astype(np.float32)
    inv_row = jnp.asarray(np.tile(inv, LANES // half).reshape(1, LANES))
    inv_col = jnp.asarray(inv.reshape(half, 1))

    fmod = _ada(c, w_final_ada, b_final_ada)
    f_shift = fmod[:, :d].reshape(bsz, 1, d)
    f_scale = fmod[:, d:].reshape(bsz, 1, d)

    qk, gv_w, lr_w = 2 * GLA_HEADS * GLA_DK, GLA_HEADS * GLA_DV, 2 * GLA_LOWRANK
    o_gq, o_gk, o_gv, o_gr = 0, qk // 2, qk, qk + gv_w
    o_lr = o_gr + gv_w
    o_dq = o_lr + lr_w
    o_dk, o_dv, o_ga, o_gb = o_dq + d, o_dq + 2 * d, o_dq + 3 * d, o_dq + 4 * d

    for l in range(depth):
        lambda_init = 0.8 - 0.6 * math.exp(-0.3 * l)
        mod = _ada(c, w_ada[l], b_ada[l]).reshape(bsz, 1, N_ADA * d)
        sh1, sc1, gt1, sh2, sc2, gt2 = (mod[:, :, i * d:(i + 1) * d] for i in range(N_ADA))

        w = w_in[l]
        col = lambda c0, n: w[:, c0:c0 + n]
        wn = jnp.concatenate([col(o_gq, qk // 2), col(o_gk, qk // 2), col(o_gv, gv_w), col(o_gr, gv_w),
                              col(o_dk, d), col(o_ga, d), col(o_gb, d)], axis=1).astype(BF16)
        wlr = jnp.zeros((d, LANES), F32).at[:, :lr_w].set(col(o_lr, lr_w)).astype(BF16)
        wt = jnp.concatenate([col(o_dq, d), col(o_dv, d), col(o_gv, gv_w)], axis=1).T.astype(BF16)
        (gq, gk, gv, gr, lr, dk, ga, gb, dqt, dvt, gvt) = _inproj(
            x, sh1, sc1, norm1_g[l].reshape(1, d), positions, wn, wlr, wt, inv_row, inv_col, tm)

        o_f, o_b = _gla(gq, gk, gv, gvt, lr, gla_wa_fw[l], gla_ba_fw[l], gla_wa_bw[l], gla_ba_bw[l], _tile(s, 512))
        o_d = _attn(dqt, dk, dvt, diff_lq1[l], diff_lk1[l], diff_lq2[l], diff_lk2[l], lambda_init,
                    _tile(s, 256), _tile(s, 512))
        x1 = _merge(o_f, o_b, gr, o_d, ga, gb, x, gt1, gla_norm_g[l], diff_norm_g[l],
                    w_gla_proj[l].astype(BF16), w_diff_proj[l].astype(BF16), w_out[l].astype(BF16), lambda_init, tm)

        sk = peer_subkeys[l].reshape(2 * PEER_HEADS, PEER_NKEYS, -1).astype(BF16)
        h2t, rb, wb, cnt, wa = _route(x1, sh2, sc2, norm2_g[l].reshape(1, d), peer_wq[l].T.astype(BF16), sk,
                                      _tile(s, 256))
        last = l == depth - 1
        assert last, "multi-layer stacks need the final norm split out of the expert kernel"
        x = _dense(h2t, peer_u[l].astype(BF16), peer_v[l].T.astype(BF16), rb, wb, cnt, wa, x1, gt2,
                   normf_g, f_shift, f_scale, tm, 2 * PEER_NKEYS)
    return x
```

```python
import functools
import math

import numpy as np
import jax
import jax.numpy as jnp
from jax import lax
from jax.experimental import pallas as pl
from jax.experimental.pallas import tpu as pltpu

F32 = jnp.float32
BF16 = jnp.bfloat16

RMS_EPS = 1e-6
ROPE_THETA = 10000.0
GLA_HEADS = 4
GLA_DK = 128
GLA_DV = 256
GLA_LOWRANK = 16
GLA_GATE_NORM = 16.0
GLA_CHUNK = 64
DIFF_HEADS = 8
DIFF_HD = 64
DIFF_VD = 128
DIFF_VROWS = DIFF_VD + 16
PEER_HEADS = 8
PEER_NKEYS = 128
PEER_TOPK = 16
N_ADA = 6

LANES = 128
VMEM_LIMIT = 48 * 1024 * 1024

NT_DIMS = (((1,), (1,)), ((), ()))


def _cparams(*sem):
    return pltpu.CompilerParams(dimension_semantics=sem, vmem_limit_bytes=VMEM_LIMIT)


def _dot(a, b):
    return jnp.dot(a, b, preferred_element_type=F32)


def _dot_nt(a, b):
    return lax.dot_general(a, b, NT_DIMS, preferred_element_type=F32)


def _rms(x, g):
    ms = jnp.mean(x * x, axis=-1, keepdims=True)
    return x * lax.rsqrt(ms + RMS_EPS) * g


def _resident(shape):
    nd = len(shape)
    return pl.BlockSpec(shape, lambda *_: (0,) * nd)


def _ada_kernel(c_ref, w_ref, b_ref, o_ref):
    c = c_ref[...]
    ca = c * jax.nn.sigmoid(c)
    o_ref[...] = jnp.dot(ca, w_ref[...], preferred_element_type=F32,
                         precision=lax.Precision.HIGHEST) + b_ref[...]


def _ada(c, w, b):
    bsz, d = c.shape
    n = w.shape[1]
    tn = 2048
    return pl.pallas_call(
        _ada_kernel,
        grid=(n // tn,),
        in_specs=[pl.BlockSpec((bsz, d), lambda j: (0, 0)),
                  pl.BlockSpec((d, tn), lambda j: (0, j)),
                  pl.BlockSpec((1, tn), lambda j: (0, j))],
        out_specs=pl.BlockSpec((bsz, tn), lambda j: (0, j)),
        out_shape=jax.ShapeDtypeStruct((bsz, n), F32),
        compiler_params=_cparams("arbitrary"),
        name="ada",
    )(c, w, b.reshape(1, n))


_C_GQ, _C_GK, _C_GV, _C_GR, _C_DK, _C_GA, _C_GB, _C_END = 0, 512, 1024, 2048, 3072, 4096, 5120, 6144
_R_DQ, _R_DV, _R_GV, _R_END = 0, 1024, 2048, 3072


def _inproj_kernel(x_ref, sh_ref, sc_ref, g_ref, pos_ref, post_ref, invr_ref, invc_ref,
                   wn_ref, wlr_ref, wt_ref,
                   gq_ref, gk_ref, gv_ref, gr_ref, lr_ref, dk_ref, ga_ref, gb_ref,
                   dqt_ref, dvt_ref, gvt_ref):
    tm = x_ref.shape[1]
    h = _rms(x_ref[0], g_ref[...]) * (1.0 + sc_ref[0]) + sh_ref[0]
    hb = h.astype(BF16)

    def proj(c0, c1):
        return _dot(hb, wn_ref[:, c0:c1])

    gq_ref[0] = (proj(_C_GQ, _C_GK) * (GLA_DK ** -0.5)).astype(BF16)
    gk_ref[0] = proj(_C_GK, _C_GV).astype(BF16)
    gv_ref[0] = proj(_C_GV, _C_GR).astype(BF16)
    gr_ref[0] = proj(_C_GR, _C_DK).astype(BF16)
    ga_ref[0] = proj(_C_GA, _C_GB).astype(BF16)
    gb_ref[0] = proj(_C_GB, _C_END).astype(BF16)
    lr_ref[0] = _dot(hb, wlr_ref[...])

    ang = pos_ref[0] * invr_ref[...]
    cs = jnp.cos(ang)
    sn = jnp.sin(ang)
    lane = lax.broadcasted_iota(jnp.int32, (tm, LANES), 1)
    first = (lane % DIFF_HD) < (DIFF_HD // 2)
    sn = jnp.where(first, -sn, sn)
    y = proj(_C_DK, _C_GA)
    for hd in range(DIFF_HEADS):
        yb = y[:, hd * LANES:(hd + 1) * LANES]
        rot = jnp.where(first, pltpu.roll(yb, LANES - DIFF_HD // 2, 1), pltpu.roll(yb, DIFF_HD // 2, 1))
        dk_ref[0, :, hd * LANES:(hd + 1) * LANES] = (yb * cs + rot * sn).astype(BF16)

    angt = invc_ref[...] * post_ref[0]
    ct = jnp.cos(angt)
    st = jnp.sin(angt)
    half = DIFF_HD // 2
    yt = _dot_nt(wt_ref[_R_DQ:_R_DV, :], hb)
    qscale = DIFF_HD ** -0.5 * math.log2(math.e)
    for blk in range(2 * DIFF_HEADS):
        r0 = blk * DIFF_HD
        x1 = yt[r0:r0 + half]
        x2 = yt[r0 + half:r0 + DIFF_HD]
        dqt_ref[0, r0:r0 + half, :] = ((x1 * ct - x2 * st) * qscale).astype(BF16)
        dqt_ref[0, r0 + half:r0 + DIFF_HD, :] = ((x2 * ct + x1 * st) * qscale).astype(BF16)
    vt = _dot_nt(wt_ref[_R_DV:_R_GV, :], hb).astype(BF16)
    ones = jnp.ones((DIFF_VROWS - DIFF_VD, tm), BF16)
    for hd in range(DIFF_HEADS):
        dvt_ref[0, hd * DIFF_VROWS:hd * DIFF_VROWS + DIFF_VD, :] = vt[hd * DIFF_VD:(hd + 1) * DIFF_VD]
        dvt_ref[0, hd * DIFF_VROWS + DIFF_VD:(hd + 1) * DIFF_VROWS, :] = ones
    gvt_ref[0] = _dot_nt(wt_ref[_R_GV:_R_END, :], hb).astype(BF16)


def _inproj(x, sh, sc, g, pos, wn, wlr, wt, inv_row, inv_col, tm):
    bsz, s, d = x.shape
    posf = pos.astype(F32)
    nat = lambda n: pl.BlockSpec((1, tm, n), lambda b, i: (b, i, 0))
    tr = lambda n: pl.BlockSpec((1, n, tm), lambda b, i: (b, 0, i))
    vec = pl.BlockSpec((1, 1, d), lambda b, i: (b, 0, 0))
    shp = lambda *dims: jax.ShapeDtypeStruct(dims, BF16)
    return pl.pallas_call(
        _inproj_kernel,
        grid=(bsz, s // tm),
        in_specs=[nat(d), vec, vec, _resident((1, d)),
                  pl.BlockSpec((1, tm, 1), lambda b, i: (b, i, 0)),
                  pl.BlockSpec((1, 1, tm), lambda b, i: (b, 0, i)),
                  _resident(inv_row.shape), _resident(inv_col.shape),
                  _resident(wn.shape), _resident(wlr.shape), _resident(wt.shape)],
        out_specs=[nat(512), nat(512), nat(1024), nat(1024), nat(LANES), nat(1024), nat(1024), nat(1024),
                   tr(1024), tr(DIFF_HEADS * DIFF_VROWS), tr(1024)],
        out_shape=[shp(bsz, s, 512), shp(bsz, s, 512), shp(bsz, s, 1024), shp(bsz, s, 1024),
                   jax.ShapeDtypeStruct((bsz, s, LANES), F32),
                   shp(bsz, s, 1024), shp(bsz, s, 1024), shp(bsz, s, 1024),
                   shp(bsz, 1024, s), shp(bsz, DIFF_HEADS * DIFF_VROWS, s), shp(bsz, 1024, s)],
        compiler_params=_cparams("arbitrary", "arbitrary"),
        name="inproj",
    )(x, sh, sc, g, posf.reshape(bsz, s, 1), posf.reshape(bsz, 1, s), inv_row, inv_col, wn, wlr, wt)


def _gla_direction(q_ref, k_ref, v_ref, vt_ref, lr_ref, wh_ref, wl_ref, ba_ref, tri_ref, o_ref, s_ref, reverse):
    blk = q_ref.shape[1]
    lr = lr_ref[0]
    lr_hi = lr.astype(BF16)
    lr_lo = (lr - lr_hi.astype(F32)).astype(BF16)
    z = _dot(lr_hi, wh_ref[...]) + _dot(lr_lo, wh_ref[...]) + _dot(lr_hi, wl_ref[...]) + ba_ref[...]
    logg = -(jnp.maximum(-z, 0.0) + jnp.log1p(jnp.exp(-jnp.abs(z)))) * (1.0 / GLA_GATE_NORM)
    lg_hi = logg.astype(BF16)
    lg_lo = (logg - lg_hi.astype(F32)).astype(BF16)
    tri = tri_ref[...]
    b = _dot(tri, lg_hi) + _dot(tri, lg_lo)
    q = q_ref[0].astype(F32)
    k = k_ref[0].astype(F32)
    v = v_ref[0]
    qe = (q * jnp.exp(b)).astype(BF16)
    ke = (k * jnp.exp(-b)).astype(BF16)
    att = jnp.where(tri > 0, _dot_nt(qe, ke), 0.0).astype(BF16)
    o_intra = _dot(att, v)

    state = s_ref[...]
    nchunk = blk // GLA_CHUNK
    order = range(nchunk - 1, -1, -1) if reverse else range(nchunk)
    for n in order:
        r0 = n * GLA_CHUNK
        edge = r0 if reverse else r0 + GLA_CHUNK - 1
        b_c = b[r0:r0 + GLA_CHUNK]
        b_e = b[edge:edge + 1]
        kend = (k[r0:r0 + GLA_CHUNK] * jnp.exp(b_e - b_c)).astype(BF16)
        o_inter = _dot_nt(qe[r0:r0 + GLA_CHUNK], state.astype(BF16))
        o_ref[0, r0:r0 + GLA_CHUNK, :] = (o_intra[r0:r0 + GLA_CHUNK] + o_inter).astype(BF16)
        state = state * jnp.exp(b_e) + _dot(vt_ref[0, :, r0:r0 + GLA_CHUNK], kend)
    s_ref[...] = state


def _gla_kernel(qf, kf, vf, vtf, lrf, qb, kb, vb, vtb, lrb,
                wfh, wfl, baf, wbh, wbl, bab, trif, trib,
                of_ref, ob_ref, sf_ref, sb_ref):
    @pl.when(pl.program_id(2) == 0)
    def _():
        sf_ref[...] = jnp.zeros_like(sf_ref)
        sb_ref[...] = jnp.zeros_like(sb_ref)

    _gla_direction(qf, kf, vf, vtf, lrf, wfh, wfl, baf, trif, of_ref, sf_ref, reverse=False)
    _gla_direction(qb, kb, vb, vtb, lrb, wbh, wbl, bab, trib, ob_ref, sb_ref, reverse=True)


def _gla(gq, gk, gv, gvt, lr, wa_f, ba_f, wa_b, ba_b, blk):
    bsz, s, _ = gq.shape
    nblk = s // blk
    def pad_rows(w, r0):
        return jnp.zeros((LANES, w.shape[1]), F32).at[r0:r0 + GLA_LOWRANK].set(w)

    def hi_lo(w):
        hi = w.astype(BF16)
        return hi, (w - hi.astype(F32)).astype(BF16)

    wfh, wfl = hi_lo(pad_rows(wa_f, 0))
    wbh, wbl = hi_lo(pad_rows(wa_b, GLA_LOWRANK))
    r = np.arange(blk)
    same = (r[:, None] // GLA_CHUNK) == (r[None, :] // GLA_CHUNK)
    tri_f = jnp.asarray(same & (r[None, :] <= r[:, None]), BF16)
    tri_b = jnp.asarray(same & (r[None, :] >= r[:, None]), BF16)

    fw = lambda b, h, i: (b, i, h)
    bw = lambda b, h, i: (b, nblk - 1 - i, h)
    fwt = lambda b, h, i: (b, h, i)
    bwt = lambda b, h, i: (b, h, nblk - 1 - i)

    def seq_specs(tok, tok_t):
        return [pl.BlockSpec((1, blk, GLA_DK), tok), pl.BlockSpec((1, blk, GLA_DK), tok),
                pl.BlockSpec((1, blk, GLA_DV), tok), pl.BlockSpec((1, GLA_DV, blk), tok_t),
                pl.BlockSpec((1, blk, LANES), lambda b, h, i, t=tok: (t(b, h, i)[0], t(b, h, i)[1], 0))]

    wspec = pl.BlockSpec((LANES, GLA_DK), lambda b, h, i: (0, h))
    bspec = pl.BlockSpec((1, GLA_DK), lambda b, h, i: (0, h))
    out_f = pl.BlockSpec((1, blk, GLA_DV), fw)
    out_b = pl.BlockSpec((1, blk, GLA_DV), bw)
    o_shape = jax.ShapeDtypeStruct((bsz, s, GLA_HEADS * GLA_DV), BF16)
    return pl.pallas_call(
        _gla_kernel,
        grid=(bsz, GLA_HEADS, nblk),
        in_specs=seq_specs(fw, fwt) + seq_specs(bw, bwt) + [wspec, wspec, bspec, wspec, wspec, bspec,
                                                             _resident((blk, blk)), _resident((blk, blk))],
        out_specs=[out_f, out_b],
        out_shape=[o_shape, o_shape],
        scratch_shapes=[pltpu.VMEM((GLA_DV, GLA_DK), F32), pltpu.VMEM((GLA_DV, GLA_DK), F32)],
        compiler_params=_cparams("arbitrary", "arbitrary", "arbitrary"),
        name="gla",
    )(gq, gk, gv, gvt, lr, gq, gk, gv, gvt, lr,
      wfh, wfl, ba_f.reshape(1, -1), wbh, wbl, ba_b.reshape(1, -1), tri_f, tri_b)


def _attn_kernel(lq1_ref, lk1_ref, lq2_ref, lk2_ref, qt_ref, k_ref, vt_ref, o_ref,
                 s_ref, p_ref, a_ref, mt_ref, m_ref, acc_ref, *, tk, lambda_init):
    nk = k_ref.shape[1] // tk
    qt = qt_ref[0]
    row = lax.broadcasted_iota(jnp.int32, qt.shape, 0)
    zero = jnp.zeros_like(qt)
    qmaps = (jnp.where(row < DIFF_HD, qt, zero), jnp.where(row >= DIFF_HD, qt, zero))
    m_ref[...] = jnp.full(m_ref.shape, -jnp.inf, F32)
    acc_ref[...] = jnp.zeros_like(acc_ref)

    def scores(t, slot):
        kj = k_ref[0, pl.ds(pl.multiple_of(t * tk, tk), tk), :]
        for mi in range(2):
            s = _dot(kj, qmaps[mi])
            s_ref[slot, mi] = s
            mt_ref[slot, mi] = jnp.max(s, axis=0, keepdims=True)

    def softmax(slot):
        for mi in range(2):
            m_old = m_ref[mi]
            m_new = jnp.maximum(m_old, mt_ref[slot, mi])
            a_ref[slot, mi] = jnp.exp2(m_old - m_new)
            p_ref[slot, mi] = jnp.exp2(s_ref[slot, mi] - m_new).astype(BF16)
            m_ref[mi] = m_new

    def values(t, slot):
        vtj = vt_ref[0, :, pl.ds(pl.multiple_of(t * tk, tk), tk)]
        for mi in range(2):
            acc_ref[mi] = acc_ref[mi] * a_ref[slot, mi] + _dot(vtj, p_ref[slot, mi])

    nslot = s_ref.shape[0]
    dist = nslot // 2
    assert nk % nslot == 0

    def group(t0, first, last):
        for u in range(nslot):
            softmax(u)
            if not (first and u < dist):
                values(t0 + u - dist, (u + dist) % nslot)
            if not (last and u >= nslot - dist):
                scores(t0 + u + dist, (u + dist) % nslot)

    for t in range(dist):
        scores(t, t)
    ngroup = nk // nslot
    group(0, True, ngroup == 1)
    if ngroup > 2:
        def middle(g, carry):
            group(g * nslot, False, False)
            return carry
        lax.fori_loop(1, ngroup - 1, middle, 0)
    if ngroup > 1:
        group(nk - nslot, False, True)
    for t in range(nk - dist, nk):
        values(t, t % nslot)
    lam = (jnp.exp(jnp.sum(lq1_ref[...] * lk1_ref[...], axis=-1, keepdims=True))
           - jnp.exp(jnp.sum(lq2_ref[...] * lk2_ref[...], axis=-1, keepdims=True)) + lambda_init)
    num = [acc_ref[mi, 0:DIFF_VD, :] for mi in range(2)]
    den = [acc_ref[mi, DIFF_VD:DIFF_VD + 1, :] for mi in range(2)]
    ot = num[0] * (1.0 / den[0]) - lam * (num[1] * (1.0 / den[1]))
    o_ref[0] = ot.T.astype(BF16)


def _attn(dqt, dk, dvt, lq1, lk1, lq2, lk2, lambda_init, tq, tk):
    bsz, s, _ = dk.shape
    lspec = _resident((1, DIFF_HD))
    return pl.pallas_call(
        functools.partial(_attn_kernel, tk=tk, lambda_init=lambda_init),
        grid=(bsz, DIFF_HEADS, s // tq),
        in_specs=[lspec, lspec, lspec, lspec,
                  pl.BlockSpec((1, 2 * DIFF_HD, tq), lambda b, h, i: (b, h, i)),
                  pl.BlockSpec((1, s, 2 * DIFF_HD), lambda b, h, i: (b, 0, h)),
                  pl.BlockSpec((1, DIFF_VROWS, s), lambda b, h, i: (b, h, 0))],
        out_specs=pl.BlockSpec((1, tq, DIFF_VD), lambda b, h, i: (b, i, h)),
        out_shape=jax.ShapeDtypeStruct((bsz, s, DIFF_HEADS * DIFF_VD), BF16),
        scratch_shapes=[pltpu.VMEM((4, 2, tk, tq), F32), pltpu.VMEM((4, 2, tk, tq), BF16),
                        pltpu.VMEM((4, 2, 1, tq), F32), pltpu.VMEM((4, 2, 1, tq), F32),
                        pltpu.VMEM((2, 1, tq), F32),
                        pltpu.VMEM((2, DIFF_VROWS, tq), F32)],
        compiler_params=_cparams("arbitrary", "arbitrary", "arbitrary"),
        name="attn",
    )(lq1.reshape(1, -1), lk1.reshape(1, -1), lq2.reshape(1, -1), lk2.reshape(1, -1), dqt, dk, dvt)


def _merge_kernel(of_ref, ob_ref, gr_ref, od_ref, ga_ref, gb_ref, x_ref, gt_ref, gg_ref, dg_ref,
                  wg_ref, wd_ref, wo_ref, o_ref, *, lambda_init):
    og = of_ref[0].astype(F32) + ob_ref[0].astype(F32)
    gg = gg_ref[...]
    og = jnp.concatenate(
        [_rms(og[:, h * GLA_DV:(h + 1) * GLA_DV], gg[:, h * GLA_DV:(h + 1) * GLA_DV]) for h in range(GLA_HEADS)],
        axis=-1)
    gr = gr_ref[0].astype(F32)
    y_gla = _dot((og * (gr * jax.nn.sigmoid(gr))).astype(BF16), wg_ref[...])
    od = od_ref[0].astype(F32)
    dg = dg_ref[...]
    od = jnp.concatenate(
        [_rms(od[:, h * DIFF_VD:(h + 1) * DIFF_VD], dg[:, h * DIFF_VD:(h + 1) * DIFF_VD]) for h in range(DIFF_HEADS)],
        axis=-1) * (1.0 - lambda_init)
    y_diff = _dot(od.astype(BF16), wd_ref[...])
    merged = (jax.nn.sigmoid(ga_ref[0].astype(F32)) * y_gla + jax.nn.sigmoid(gb_ref[0].astype(F32)) * y_diff)
    o_ref[0] = x_ref[0] + gt_ref[0] * _dot(merged.astype(BF16), wo_ref[...])


def _merge(o_f, o_b, gr, o_d, ga, gb, x, gt1, gla_g, diff_g, wg, wd, wo, lambda_init, tm):
    bsz, s, d = x.shape
    tile = pl.BlockSpec((1, tm, d), lambda b, i: (b, i, 0))
    vec = pl.BlockSpec((1, 1, d), lambda b, i: (b, 0, 0))
    return pl.pallas_call(
        functools.partial(_merge_kernel, lambda_init=lambda_init),
        grid=(bsz, s // tm),
        in_specs=[tile] * 7 + [vec, _resident((1, d)), _resident((1, d)),
                               _resident((d, d)), _resident((d, d)), _resident((d, d))],
        out_specs=tile,
        out_shape=jax.ShapeDtypeStruct((bsz, s, d), F32),
        compiler_params=_cparams("arbitrary", "arbitrary"),
        name="merge",
    )(o_f, o_b, gr, o_d, ga, gb, x, gt1, gla_g.reshape(1, d), diff_g.reshape(1, d), wg, wd, wo)


_CAND_ROW_COUNT = (16, 8, 5, 4, 3, 2, 2, 2)
_SUB = 8


def _top16(a):
    work = a
    rank = jnp.full(a.shape, float(PEER_TOPK), F32)
    vals = []
    for r in range(PEER_TOPK):
        m = jnp.max(work, axis=0, keepdims=True)
        vals.append(m)
        hit = work == m
        rank = jnp.where(hit, float(r), rank)
        work = jnp.where(hit, -jnp.inf, work)
    return jnp.concatenate(vals, axis=0), rank


def _route_head(a, b):
    t = a.shape[1]
    top_a, rank_a = _top16(a)
    top_b, rank_b = _top16(b)
    rowi = lax.broadcasted_iota(jnp.int32, (_SUB, t), 0)
    pieces = [top_b + top_a[0:1]]
    for r in range(1, _SUB):
        piece = top_b[0:_SUB] + top_a[r:r + 1]
        n = _CAND_ROW_COUNT[r]
        pieces.append(piece if n == _SUB else jnp.where(rowi < n, piece, -jnp.inf))
    pieces.append(top_a[_SUB:PEER_TOPK] + top_b[0:1])
    cand = jnp.concatenate(pieces, axis=0)
    work = cand
    thr = None
    for _ in range(PEER_TOPK):
        thr = jnp.max(work, axis=0, keepdims=True)
        work = jnp.where(work == thr, -jnp.inf, work)
    sel = cand >= thr
    smax = top_a[0:1] + top_b[0:1]
    z = jnp.sum(jnp.where(sel, jnp.exp(cand - smax), 0.0), axis=0, keepdims=True)
    self32 = sel.astype(F32)
    counts = [jnp.sum(self32[0:PEER_TOPK], axis=0, keepdims=True)]
    for r in range(1, _SUB):
        r0 = PEER_TOPK + (r - 1) * _SUB
        counts.append(jnp.sum(self32[r0:r0 + _SUB], axis=0, keepdims=True))
    tail0 = PEER_TOPK + (_SUB - 1) * _SUB
    for r in range(_SUB, PEER_TOPK):
        counts.append(self32[tail0 + r - _SUB:tail0 + r - _SUB + 1])
    cnt = jnp.zeros(a.shape, F32)
    for r in range(PEER_TOPK):
        cnt = jnp.where(rank_a == float(r), counts[r], cnt)
    wa = jnp.exp(a - top_a[0:1]) * (1.0 / z)
    wb = jnp.exp(b - top_b[0:1])
    return rank_b, wb, cnt, wa


def _route_kernel(x_ref, sh_ref, sc_ref, g_ref, wqt_ref, sk_ref, h2t_ref, rb_ref, wb_ref, cnt_ref, wa_ref):
    h2 = _rms(x_ref[0], g_ref[...]) * (1.0 + sc_ref[0]) + sh_ref[0]
    hb = h2.astype(BF16)
    h2t_ref[0] = h2.T.astype(BF16)
    qt = _dot_nt(wqt_ref[...], hb).astype(BF16)
    for h in range(PEER_HEADS):
        g0 = 2 * h * PEER_NKEYS
        a = _dot(sk_ref[2 * h], qt[g0:g0 + PEER_NKEYS])
        b = _dot(sk_ref[2 * h + 1], qt[g0 + PEER_NKEYS:g0 + 2 * PEER_NKEYS])
        rank_b, wb, cnt, wa = _route_head(a, b)
        rb_ref[0, h] = rank_b.astype(BF16)
        wb_ref[0, h] = wb.astype(BF16)
        cnt_ref[0, h] = cnt
        wa_ref[0, h] = wa


def _route(x1, sh2, sc2, g, wqt, sk, tm):
    bsz, s, d = x1.shape
    vec = pl.BlockSpec((1, 1, d), lambda b, i: (b, 0, 0))
    gate = pl.BlockSpec((1, PEER_HEADS, PEER_NKEYS, tm), lambda b, i: (b, 0, 0, i))
    gshape = lambda dt: jax.ShapeDtypeStruct((bsz, PEER_HEADS, PEER_NKEYS, s), dt)
    return pl.pallas_call(
        _route_kernel,
        grid=(bsz, s // tm),
        in_specs=[pl.BlockSpec((1, tm, d), lambda b, i: (b, i, 0)), vec, vec, _resident((1, d)),
                  _resident(wqt.shape), _resident(sk.shape)],
        out_specs=[pl.BlockSpec((1, d, tm), lambda b, i: (b, 0, i)), gate, gate, gate, gate],
        out_shape=[jax.ShapeDtypeStruct((bsz, d, s), BF16), gshape(BF16), gshape(BF16), gshape(F32), gshape(F32)],
        compiler_params=_cparams("arbitrary", "arbitrary"),
        name="route",
    )(x1, sh2, sc2, g, wqt, sk)


def _dense_kernel(h2t_ref, u_ref, vt_ref, rb_ref, wb_ref, cnt_ref, wa_ref, x_ref, gt_ref, fg_ref, fsh_ref, fsc_ref,
                  o_ref, acc_ref, *, eblk):
    e = pl.program_id(2)

    @pl.when(e == 0)
    def _():
        acc_ref[...] = jnp.zeros_like(acc_ref)

    at = _dot(u_ref[...], h2t_ref[0])
    act = 0.5 * at * (1.0 + lax.erf(at * math.sqrt(0.5)))
    gates = []
    for ii in range(eblk // PEER_NKEYS):
        i = e * (eblk // PEER_NKEYS) + ii
        w = None
        for h in range(PEER_HEADS):
            cnt_i = cnt_ref[0, h, pl.ds(i, 1), :].astype(BF16)
            wa_i = wa_ref[0, h, pl.ds(i, 1), :].astype(BF16)
            wh = jnp.where(rb_ref[0, h] < cnt_i, wb_ref[0, h], jnp.zeros((), BF16)) * wa_i
            w = wh if w is None else w + wh
        gates.append(w)
    gate = jnp.concatenate(gates, axis=0)
    p = (act * gate.astype(F32)).astype(BF16)
    acc_ref[...] += _dot(vt_ref[0], p)

    @pl.when(e == pl.num_programs(2) - 1)
    def _():
        y = x_ref[0] + gt_ref[0] * acc_ref[...].T
        o_ref[0] = _rms(y, fg_ref[...]) * (1.0 + fsc_ref[0]) + fsh_ref[0]


def _dense(h2t, u, vt, rb, wb, cnt, wa, x1, gt2, fg, fsh, fsc, tm, eblk):
    bsz, s, d = x1.shape
    ne = u.shape[0]
    gate = pl.BlockSpec((1, PEER_HEADS, PEER_NKEYS, tm), lambda b, i, e: (b, 0, 0, i))
    tile = pl.BlockSpec((1, tm, d), lambda b, i, e: (b, i, 0))
    vec = pl.BlockSpec((1, 1, d), lambda b, i, e: (b, 0, 0))
    return pl.pallas_call(
        functools.partial(_dense_kernel, eblk=eblk),
        grid=(bsz, s // tm, ne // eblk),
        in_specs=[pl.BlockSpec((1, d, tm), lambda b, i, e: (b, 0, i)),
                  pl.BlockSpec((eblk, d), lambda b, i, e: (e, 0)),
                  pl.BlockSpec((1, d, eblk), lambda b, i, e: (e, 0, 0)),
                  gate, gate, gate, gate, tile, vec, _resident((1, d)), vec, vec],
        out_specs=tile,
        out_shape=jax.ShapeDtypeStruct((bsz, s, d), F32),
        scratch_shapes=[pltpu.VMEM((d, tm), F32)],
        compiler_params=_cparams("arbitrary", "arbitrary", "arbitrary"),
        name="dense",
    )(h2t, u, vt, rb, wb, cnt, wa, x1, gt2, fg.reshape(1, d), fsh, fsc)


def _tile(s, want):
    return want if s % want == 0 else s


def kernel(x, c, positions, w_ada, b_ada, norm1_g, w_in, gla_wa_fw, gla_ba_fw, gla_wa_bw, gla_ba_bw, gla_norm_g,
           diff_lq1, diff_lk1, diff_lq2, diff_lk2, diff_norm_g, w_gla_proj, w_diff_proj, w_out, norm2_g, peer_wq,
           peer_subkeys, peer_u, peer_v, w_final_ada, b_final_ada, normf_g):
    bsz, s, d = x.shape
    depth = w_ada.shape[0]
    tm = _tile(s, 512)

    half = DIFF_HD // 2
    inv = (ROPE_THETA ** (-np.arange(0, DIFF_HD, 2, dtype=np.float32) / DIFF_HD)).astype(np.float32)
    inv_row = jnp.asarray(np.tile(inv, LANES // half).reshape(1, LANES))
    inv_col = jnp.asarray(inv.reshape(half, 1))

    fmod = _ada(c, w_final_ada, b_final_ada)
    f_shift = fmod[:, :d].reshape(bsz, 1, d)
    f_scale = fmod[:, d:].reshape(bsz, 1, d)

    qk, gv_w, lr_w = 2 * GLA_HEADS * GLA_DK, GLA_HEADS * GLA_DV, 2 * GLA_LOWRANK
    o_gq, o_gk, o_gv, o_gr = 0, qk // 2, qk, qk + gv_w
    o_lr = o_gr + gv_w
    o_dq = o_lr + lr_w
    o_dk, o_dv, o_ga, o_gb = o_dq + d, o_dq + 2 * d, o_dq + 3 * d, o_dq + 4 * d

    for l in range(depth):
        lambda_init = 0.8 - 0.6 * math.exp(-0.3 * l)
        mod = _ada(c, w_ada[l], b_ada[l]).reshape(bsz, 1, N_ADA * d)
        sh1, sc1, gt1, sh2, sc2, gt2 = (mod[:, :, i * d:(i + 1) * d] for i in range(N_ADA))

        w = w_in[l]
        col = lambda c0, n: w[:, c0:c0 + n]
        wn = jnp.concatenate([col(o_gq, qk // 2), col(o_gk, qk // 2), col(o_gv, gv_w), col(o_gr, gv_w),
                              col(o_dk, d), col(o_ga, d), col(o_gb, d)], axis=1).astype(BF16)
        wlr = jnp.zeros((d, LANES), F32).at[:, :lr_w].set(col(o_lr, lr_w)).astype(BF16)
        wt = jnp.concatenate([col(o_dq, d), col(o_dv, d), col(o_gv, gv_w)], axis=1).T.astype(BF16)
        (gq, gk, gv, gr, lr, dk, ga, gb, dqt, dvt, gvt) = _inproj(
            x, sh1, sc1, norm1_g[l].reshape(1, d), positions, wn, wlr, wt, inv_row, inv_col, tm)

        o_f, o_b = _gla(gq, gk, gv, gvt, lr, gla_wa_fw[l], gla_ba_fw[l], gla_wa_bw[l], gla_ba_bw[l], _tile(s, 512))
        o_d = _attn(dqt, dk, dvt, diff_lq1[l], diff_lk1[l], diff_lq2[l], diff_lk2[l], lambda_init,
                    _tile(s, 256), _tile(s, 512))
        x1 = _merge(o_f, o_b, gr, o_d, ga, gb, x, gt1, gla_norm_g[l], diff_norm_g[l],
                    w_gla_proj[l].astype(BF16), w_diff_proj[l].astype(BF16), w_out[l].astype(BF16), lambda_init, tm)

        sk = peer_subkeys[l].reshape(2 * PEER_HEADS, PEER_NKEYS, -1).astype(BF16)
        h2t, rb, wb, cnt, wa = _route(x1, sh2, sc2, norm2_g[l].reshape(1, d), peer_wq[l].T.astype(BF16), sk,
                                      _tile(s, 256))
        last = l == depth - 1
        assert last, "multi-layer stacks need the final norm split out of the expert kernel"
        eblk = 4 * PEER_NKEYS
        vt = peer_v[l].astype(BF16).reshape(-1, eblk, d).transpose(0, 2, 1)
        x = _dense(h2t, peer_u[l].astype(BF16), vt, rb, wb, cnt, wa, x1, gt2,
                   normf_g, f_shift, f_scale, tm, eblk)
    return x
```

```python
import functools
import math

import numpy as np
import jax
import jax.numpy as jnp
from jax import lax
from jax.experimental import pallas as pl
from jax.experimental.pallas import tpu as pltpu

F32 = jnp.float32
BF16 = jnp.bfloat16

RMS_EPS = 1e-6
ROPE_THETA = 10000.0
GLA_HEADS = 4
GLA_DK = 128
GLA_DV = 256
GLA_LOWRANK = 16
GLA_GATE_NORM = 16.0
GLA_CHUNK = 64
DIFF_HEADS = 8
DIFF_HD = 64
DIFF_VD = 128
DIFF_VROWS = DIFF_VD + 16
PEER_HEADS = 8
PEER_NKEYS = 128
PEER_TOPK = 16
N_ADA = 6

LANES = 128
VMEM_LIMIT = 48 * 1024 * 1024

NT_DIMS = (((1,), (1,)), ((), ()))


def _cparams(*sem):
    return pltpu.CompilerParams(dimension_semantics=sem, vmem_limit_bytes=VMEM_LIMIT)


def _dot(a, b):
    return jnp.dot(a, b, preferred_element_type=F32)


def _dot_nt(a, b):
    return lax.dot_general(a, b, NT_DIMS, preferred_element_type=F32)


def _rms(x, g):
    ms = jnp.mean(x * x, axis=-1, keepdims=True)
    return x * lax.rsqrt(ms + RMS_EPS) * g


def _resident(shape):
    nd = len(shape)
    return pl.BlockSpec(shape, lambda *_: (0,) * nd)


def _ada_kernel(c_ref, w_ref, b_ref, o_ref):
    c = c_ref[...]
    ca = c * jax.nn.sigmoid(c)
    o_ref[...] = jnp.dot(ca, w_ref[...], preferred_element_type=F32,
                         precision=lax.Precision.HIGHEST) + b_ref[...]


def _ada(c, w, b):
    bsz, d = c.shape
    n = w.shape[1]
    tn = 2048
    return pl.pallas_call(
        _ada_kernel,
        grid=(n // tn,),
        in_specs=[pl.BlockSpec((bsz, d), lambda j: (0, 0)),
                  pl.BlockSpec((d, tn), lambda j: (0, j)),
                  pl.BlockSpec((1, tn), lambda j: (0, j))],
        out_specs=pl.BlockSpec((bsz, tn), lambda j: (0, j)),
        out_shape=jax.ShapeDtypeStruct((bsz, n), F32),
        compiler_params=_cparams("arbitrary"),
        name="ada",
    )(c, w, b.reshape(1, n))


_C_GQ, _C_GK, _C_GV, _C_GR, _C_DK, _C_GA, _C_GB, _C_END = 0, 512, 1024, 2048, 3072, 4096, 5120, 6144
_R_DQ, _R_DV, _R_GV, _R_END = 0, 1024, 2048, 3072


def _inproj_kernel(x_ref, sh_ref, sc_ref, g_ref, pos_ref, post_ref, invr_ref, invc_ref,
                   wn_ref, wlr_ref, wt_ref,
                   gq_ref, gk_ref, gv_ref, gr_ref, lr_ref, dk1_ref, dk2_ref, ga_ref, gb_ref,
                   dqt_ref, dvt_ref, gvt_ref):
    tm = x_ref.shape[1]
    h = _rms(x_ref[0], g_ref[...]) * (1.0 + sc_ref[0]) + sh_ref[0]
    hb = h.astype(BF16)

    def proj(c0, c1):
        return _dot(hb, wn_ref[:, c0:c1])

    gq_ref[0] = (proj(_C_GQ, _C_GK) * (GLA_DK ** -0.5)).astype(BF16)
    gk_ref[0] = proj(_C_GK, _C_GV).astype(BF16)
    gv_ref[0] = proj(_C_GV, _C_GR).astype(BF16)
    gr_ref[0] = proj(_C_GR, _C_DK).astype(BF16)
    ga_ref[0] = proj(_C_GA, _C_GB).astype(BF16)
    gb_ref[0] = proj(_C_GB, _C_END).astype(BF16)
    lr_ref[0] = _dot(hb, wlr_ref[...])

    ang = pos_ref[0] * invr_ref[...]
    cs = jnp.cos(ang)
    sn = jnp.sin(ang)
    lane = lax.broadcasted_iota(jnp.int32, (tm, LANES), 1)
    first = (lane % DIFF_HD) < (DIFF_HD // 2)
    sn = jnp.where(first, -sn, sn)
    y = proj(_C_DK, _C_GA)
    pad = jnp.where(lane == DIFF_HD, 1.0, 0.0)
    for hd in range(DIFF_HEADS):
        yb = y[:, hd * LANES:(hd + 1) * LANES]
        rot = jnp.where(first, pltpu.roll(yb, LANES - DIFF_HD // 2, 1), pltpu.roll(yb, DIFF_HD // 2, 1))
        kr = yb * cs + rot * sn
        dk1_ref[0, :, hd * LANES:(hd + 1) * LANES] = jnp.where(lane < DIFF_HD, kr, pad).astype(BF16)
        dk2_ref[0, :, hd * LANES:(hd + 1) * LANES] = jnp.where(lane < DIFF_HD, pltpu.roll(kr, DIFF_HD, 1),
                                                                pad).astype(BF16)

    angt = invc_ref[...] * post_ref[0]
    ct = jnp.cos(angt)
    st = jnp.sin(angt)
    half = DIFF_HD // 2
    yt = _dot_nt(wt_ref[_R_DQ:_R_DV, :], hb)
    qscale = DIFF_HD ** -0.5 * math.log2(math.e)
    for blk in range(2 * DIFF_HEADS):
        r0 = blk * DIFF_HD
        x1 = yt[r0:r0 + half]
        x2 = yt[r0 + half:r0 + DIFF_HD]
        dqt_ref[0, r0:r0 + half, :] = ((x1 * ct - x2 * st) * qscale).astype(BF16)
        dqt_ref[0, r0 + half:r0 + DIFF_HD, :] = ((x2 * ct + x1 * st) * qscale).astype(BF16)
    vt = _dot_nt(wt_ref[_R_DV:_R_GV, :], hb).astype(BF16)
    ones = jnp.ones((DIFF_VROWS - DIFF_VD, tm), BF16)
    for hd in range(DIFF_HEADS):
        dvt_ref[0, hd * DIFF_VROWS:hd * DIFF_VROWS + DIFF_VD, :] = vt[hd * DIFF_VD:(hd + 1) * DIFF_VD]
        dvt_ref[0, hd * DIFF_VROWS + DIFF_VD:(hd + 1) * DIFF_VROWS, :] = ones
    gvt_ref[0] = _dot_nt(wt_ref[_R_GV:_R_END, :], hb).astype(BF16)


def _inproj(x, sh, sc, g, pos, wn, wlr, wt, inv_row, inv_col, tm):
    bsz, s, d = x.shape
    posf = pos.astype(F32)
    nat = lambda n: pl.BlockSpec((1, tm, n), lambda b, i: (b, i, 0))
    tr = lambda n: pl.BlockSpec((1, n, tm), lambda b, i: (b, 0, i))
    vec = pl.BlockSpec((1, 1, d), lambda b, i: (b, 0, 0))
    shp = lambda *dims: jax.ShapeDtypeStruct(dims, BF16)
    return pl.pallas_call(
        _inproj_kernel,
        grid=(bsz, s // tm),
        in_specs=[nat(d), vec, vec, _resident((1, d)),
                  pl.BlockSpec((1, tm, 1), lambda b, i: (b, i, 0)),
                  pl.BlockSpec((1, 1, tm), lambda b, i: (b, 0, i)),
                  _resident(inv_row.shape), _resident(inv_col.shape),
                  _resident(wn.shape), _resident(wlr.shape), _resident(wt.shape)],
        out_specs=[nat(512), nat(512), nat(1024), nat(1024), nat(LANES), nat(1024), nat(1024), nat(1024), nat(1024),
                   tr(1024), tr(DIFF_HEADS * DIFF_VROWS), tr(1024)],
        out_shape=[shp(bsz, s, 512), shp(bsz, s, 512), shp(bsz, s, 1024), shp(bsz, s, 1024),
                   jax.ShapeDtypeStruct((bsz, s, LANES), F32),
                   shp(bsz, s, 1024), shp(bsz, s, 1024), shp(bsz, s, 1024), shp(bsz, s, 1024),
                   shp(bsz, 1024, s), shp(bsz, DIFF_HEADS * DIFF_VROWS, s), shp(bsz, 1024, s)],
        compiler_params=_cparams("arbitrary", "arbitrary"),
        name="inproj",
    )(x, sh, sc, g, posf.reshape(bsz, s, 1), posf.reshape(bsz, 1, s), inv_row, inv_col, wn, wlr, wt)


def _gla_direction(q_ref, k_ref, v_ref, vt_ref, lr_ref, wh_ref, wl_ref, ba_ref, tri_ref, o_ref, s_ref, reverse):
    blk = q_ref.shape[1]
    lr = lr_ref[0]
    lr_hi = lr.astype(BF16)
    lr_lo = (lr - lr_hi.astype(F32)).astype(BF16)
    z = _dot(lr_hi, wh_ref[...]) + _dot(lr_lo, wh_ref[...]) + _dot(lr_hi, wl_ref[...]) + ba_ref[...]
    logg = -(jnp.maximum(-z, 0.0) + jnp.log1p(jnp.exp(-jnp.abs(z)))) * (1.0 / GLA_GATE_NORM)
    lg_hi = logg.astype(BF16)
    lg_lo = (logg - lg_hi.astype(F32)).astype(BF16)
    tri = tri_ref[...]
    b = _dot(tri, lg_hi) + _dot(tri, lg_lo)
    q = q_ref[0].astype(F32)
    k = k_ref[0].astype(F32)
    v = v_ref[0]
    qe = (q * jnp.exp(b)).astype(BF16)
    ke = (k * jnp.exp(-b)).astype(BF16)
    att = jnp.where(tri > 0, _dot_nt(qe, ke), 0.0).astype(BF16)
    o_intra = _dot(att, v)

    state = s_ref[...]
    nchunk = blk // GLA_CHUNK
    order = range(nchunk - 1, -1, -1) if reverse else range(nchunk)
    for n in order:
        r0 = n * GLA_CHUNK
        edge = r0 if reverse else r0 + GLA_CHUNK - 1
        b_c = b[r0:r0 + GLA_CHUNK]
        b_e = b[edge:edge + 1]
        kend = (k[r0:r0 + GLA_CHUNK] * jnp.exp(b_e - b_c)).astype(BF16)
        o_inter = _dot_nt(qe[r0:r0 + GLA_CHUNK], state.astype(BF16))
        o_ref[0, r0:r0 + GLA_CHUNK, :] = (o_intra[r0:r0 + GLA_CHUNK] + o_inter).astype(BF16)
        state = state * jnp.exp(b_e) + _dot(vt_ref[0, :, r0:r0 + GLA_CHUNK], kend)
    s_ref[...] = state


def _gla_kernel(qf, kf, vf, vtf, lrf, qb, kb, vb, vtb, lrb,
                wfh, wfl, baf, wbh, wbl, bab, trif, trib,
                of_ref, ob_ref, sf_ref, sb_ref):
    @pl.when(pl.program_id(2) == 0)
    def _():
        sf_ref[...] = jnp.zeros_like(sf_ref)
        sb_ref[...] = jnp.zeros_like(sb_ref)

    _gla_direction(qf, kf, vf, vtf, lrf, wfh, wfl, baf, trif, of_ref, sf_ref, reverse=False)
    _gla_direction(qb, kb, vb, vtb, lrb, wbh, wbl, bab, trib, ob_ref, sb_ref, reverse=True)


def _gla(gq, gk, gv, gvt, lr, wa_f, ba_f, wa_b, ba_b, blk):
    bsz, s, _ = gq.shape
    nblk = s // blk
    def pad_rows(w, r0):
        return jnp.zeros((LANES, w.shape[1]), F32).at[r0:r0 + GLA_LOWRANK].set(w)

    def hi_lo(w):
        hi = w.astype(BF16)
        return hi, (w - hi.astype(F32)).astype(BF16)

    wfh, wfl = hi_lo(pad_rows(wa_f, 0))
    wbh, wbl = hi_lo(pad_rows(wa_b, GLA_LOWRANK))
    r = np.arange(blk)
    same = (r[:, None] // GLA_CHUNK) == (r[None, :] // GLA_CHUNK)
    tri_f = jnp.asarray(same & (r[None, :] <= r[:, None]), BF16)
    tri_b = jnp.asarray(same & (r[None, :] >= r[:, None]), BF16)

    fw = lambda b, h, i: (b, i, h)
    bw = lambda b, h, i: (b, nblk - 1 - i, h)
    fwt = lambda b, h, i: (b, h, i)
    bwt = lambda b, h, i: (b, h, nblk - 1 - i)

    def seq_specs(tok, tok_t):
        return [pl.BlockSpec((1, blk, GLA_DK), tok), pl.BlockSpec((1, blk, GLA_DK), tok),
                pl.BlockSpec((1, blk, GLA_DV), tok), pl.BlockSpec((1, GLA_DV, blk), tok_t),
                pl.BlockSpec((1, blk, LANES), lambda b, h, i, t=tok: (t(b, h, i)[0], t(b, h, i)[1], 0))]

    wspec = pl.BlockSpec((LANES, GLA_DK), lambda b, h, i: (0, h))
    bspec = pl.BlockSpec((1, GLA_DK), lambda b, h, i: (0, h))
    out_f = pl.BlockSpec((1, blk, GLA_DV), fw)
    out_b = pl.BlockSpec((1, blk, GLA_DV), bw)
    o_shape = jax.ShapeDtypeStruct((bsz, s, GLA_HEADS * GLA_DV), BF16)
    return pl.pallas_call(
        _gla_kernel,
        grid=(bsz, GLA_HEADS, nblk),
        in_specs=seq_specs(fw, fwt) + seq_specs(bw, bwt) + [wspec, wspec, bspec, wspec, wspec, bspec,
                                                             _resident((blk, blk)), _resident((blk, blk))],
        out_specs=[out_f, out_b],
        out_shape=[o_shape, o_shape],
        scratch_shapes=[pltpu.VMEM((GLA_DV, GLA_DK), F32), pltpu.VMEM((GLA_DV, GLA_DK), F32)],
        compiler_params=_cparams("arbitrary", "arbitrary", "arbitrary"),
        name="gla",
    )(gq, gk, gv, gvt, lr, gq, gk, gv, gvt, lr,
      wfh, wfl, ba_f.reshape(1, -1), wbh, wbl, ba_b.reshape(1, -1), tri_f, tri_b)


ATTN_SAFE_SHIFT = 40.0
ATTN_NORM_CHUNK = 1024


def _ring(nk, nslot, stages):
    assert nk % nslot == 0
    offs = [off for _, off in stages]

    def static_iter(i):
        for fn, off in stages:
            if 0 <= i + off < nk:
                fn(i + off, (i + off) % nslot)

    def full(g):
        return all(0 <= g * nslot + u + off < nk for u in range(nslot) for off in offs)

    ngroup = nk // nslot
    interior = [g for g in range(ngroup) if full(g)]
    g_lo, g_hi = (interior[0], interior[-1] + 1) if interior else (ngroup, ngroup)
    for i in range(-max(offs), g_lo * nslot):
        static_iter(i)
    if g_hi - g_lo == 1:
        for i in range(g_lo * nslot, g_hi * nslot):
            static_iter(i)
    elif g_hi > g_lo:
        def group(g, carry):
            for u in range(nslot):
                for fn, off in stages:
                    fn(g * nslot + u + off, (u + off) % nslot)
            return carry
        lax.fori_loop(g_lo, g_hi, group, 0)
    for i in range(g_hi * nslot, nk - min(offs)):
        static_iter(i)


def _attn_kernel(lq1_ref, lk1_ref, lq2_ref, lk2_ref, qt_ref, k1_ref, k2_ref, vt_ref, o_ref,
                 s_ref, p_ref, a_ref, mt_ref, m_ref, acc_ref, kmax_ref, *, tk, lambda_init):
    k_refs = (k1_ref, k2_ref)
    nkeys = k1_ref.shape[1]
    nk = nkeys // tk
    nslot = p_ref.shape[0]
    dist = nslot // 2
    tq = qt_ref.shape[2]
    nchunk = max(nkeys // ATTN_NORM_CHUNK, 1)
    kc = nkeys // nchunk

    @pl.when(pl.program_id(2) == 0)
    def _():
        lane = lax.broadcasted_iota(jnp.int32, (kc, LANES), 1)
        for mi in range(2):
            def chunk(i, mx):
                kk = k_refs[mi][0, pl.ds(pl.multiple_of(i * kc, kc), kc), :].astype(F32)
                kk = jnp.where(lane < DIFF_HD, kk, 0.0)
                n2 = jnp.sum(kk * kk, axis=1, keepdims=True)
                return jnp.maximum(mx, jnp.max(n2, axis=0, keepdims=True))
            kmax_ref[mi] = jnp.sqrt(lax.fori_loop(0, nchunk, chunk, jnp.zeros((1, 1), F32)))

    qt = qt_ref[0]
    row = lax.broadcasted_iota(jnp.int32, (16, tq), 0)
    q_shift, q_plain, bound = [], [], []
    for mi in range(2):
        qm = qt[mi * DIFF_HD:(mi + 1) * DIFF_HD]
        qf = qm.astype(F32)
        c = jnp.sqrt(jnp.sum(qf * qf, axis=0, keepdims=True)) * kmax_ref[mi]
        c = c.astype(BF16).astype(F32)
        shift_rows = jnp.where(row == 0, -c, 0.0).astype(BF16)
        q_shift.append(jnp.concatenate([qm, shift_rows, jnp.zeros((DIFF_HD - 16, tq), BF16)], axis=0))
        q_plain.append(jnp.concatenate([qm, jnp.zeros((DIFF_HD, tq), BF16)], axis=0))
        bound.append(jnp.max(c))
    safe = jnp.maximum(bound[0], bound[1]) <= ATTN_SAFE_SHIFT
    acc_ref[...] = jnp.zeros_like(acc_ref)

    def key_tile(mi, t):
        return k_refs[mi][0, pl.ds(pl.multiple_of(t * tk, tk), tk), :]

    def value_tile(t):
        return vt_ref[0, :, pl.ds(pl.multiple_of(t * tk, tk), tk)]

    @pl.when(safe)
    def _():
        def probs(t, slot):
            for mi in range(2):
                p_ref[slot, mi] = jnp.exp2(_dot(key_tile(mi, t), q_shift[mi])).astype(BF16)

        def values(t, slot):
            vtj = value_tile(t)
            for mi in range(2):
                acc_ref[mi] += _dot(vtj, p_ref[slot, mi])

        _ring(nk, nslot, [(values, 0), (probs, dist)])

    @pl.when(jnp.logical_not(safe))
    def _():
        m_ref[...] = jnp.full(m_ref.shape, -jnp.inf, F32)

        def scores(t, slot):
            for mi in range(2):
                s = _dot(key_tile(mi, t), q_plain[mi])
                s_ref[slot, mi] = s
                mt_ref[slot, mi] = jnp.max(s, axis=0, keepdims=True)

        def softmax(t, slot):
            for mi in range(2):
                m_old = m_ref[mi]
                m_new = jnp.maximum(m_old, mt_ref[slot, mi])
                a_ref[slot, mi] = jnp.exp2(m_old - m_new)
                p_ref[slot, mi] = jnp.exp2(s_ref[slot, mi] - m_new).astype(BF16)
                m_ref[mi] = m_new

        def values(t, slot):
            vtj = value_tile(t)
            for mi in range(2):
                acc_ref[mi] = acc_ref[mi] * a_ref[slot, mi] + _dot(vtj, p_ref[slot, mi])

        _ring(nk, nslot, [(softmax, 0), (values, -dist), (scores, dist)])

    lam = (jnp.exp(jnp.sum(lq1_ref[...] * lk1_ref[...], axis=-1, keepdims=True))
           - jnp.exp(jnp.sum(lq2_ref[...] * lk2_ref[...], axis=-1, keepdims=True)) + lambda_init)
    num = [acc_ref[mi, 0:DIFF_VD, :] for mi in range(2)]
    den = [acc_ref[mi, DIFF_VD:DIFF_VD + 1, :] for mi in range(2)]
    ot = num[0] * (1.0 / den[0]) - lam * (num[1] * (1.0 / den[1]))
    o_ref[0] = ot.T.astype(BF16)


def _attn(dqt, dk1, dk2, dvt, lq1, lk1, lq2, lk2, lambda_init, tq, tk):
    bsz, s, _ = dk1.shape
    lspec = _resident((1, DIFF_HD))
    kspec = pl.BlockSpec((1, s, LANES), lambda b, h, i: (b, 0, h))
    return pl.pallas_call(
        functools.partial(_attn_kernel, tk=tk, lambda_init=lambda_init),
        grid=(bsz, DIFF_HEADS, s // tq),
        in_specs=[lspec, lspec, lspec, lspec,
                  pl.BlockSpec((1, 2 * DIFF_HD, tq), lambda b, h, i: (b, h, i)),
                  kspec, kspec,
                  pl.BlockSpec((1, DIFF_VROWS, s), lambda b, h, i: (b, h, 0))],
        out_specs=pl.BlockSpec((1, tq, DIFF_VD), lambda b, h, i: (b, i, h)),
        out_shape=jax.ShapeDtypeStruct((bsz, s, DIFF_HEADS * DIFF_VD), BF16),
        scratch_shapes=[pltpu.VMEM((4, 2, tk, tq), F32), pltpu.VMEM((4, 2, tk, tq), BF16),
                        pltpu.VMEM((4, 2, 1, tq), F32), pltpu.VMEM((4, 2, 1, tq), F32),
                        pltpu.VMEM((2, 1, tq), F32),
                        pltpu.VMEM((2, DIFF_VROWS, tq), F32),
                        pltpu.VMEM((2, 1, 1), F32)],
        compiler_params=_cparams("arbitrary", "arbitrary", "arbitrary"),
        name="attn",
    )(lq1.reshape(1, -1), lk1.reshape(1, -1), lq2.reshape(1, -1), lk2.reshape(1, -1), dqt, dk1, dk2, dvt)


def _merge_kernel(of_ref, ob_ref, gr_ref, od_ref, ga_ref, gb_ref, x_ref, gt_ref, gg_ref, dg_ref,
                  wg_ref, wd_ref, wo_ref, o_ref, *, lambda_init):
    og = of_ref[0].astype(F32) + ob_ref[0].astype(F32)
    gg = gg_ref[...]
    og = jnp.concatenate(
        [_rms(og[:, h * GLA_DV:(h + 1) * GLA_DV], gg[:, h * GLA_DV:(h + 1) * GLA_DV]) for h in range(GLA_HEADS)],
        axis=-1)
    gr = gr_ref[0].astype(F32)
    y_gla = _dot((og * (gr * jax.nn.sigmoid(gr))).astype(BF16), wg_ref[...])
    od = od_ref[0].astype(F32)
    dg = dg_ref[...]
    od = jnp.concatenate(
        [_rms(od[:, h * DIFF_VD:(h + 1) * DIFF_VD], dg[:, h * DIFF_VD:(h + 1) * DIFF_VD]) for h in range(DIFF_HEADS)],
        axis=-1) * (1.0 - lambda_init)
    y_diff = _dot(od.astype(BF16), wd_ref[...])
    merged = (jax.nn.sigmoid(ga_ref[0].astype(F32)) * y_gla + jax.nn.sigmoid(gb_ref[0].astype(F32)) * y_diff)
    o_ref[0] = x_ref[0] + gt_ref[0] * _dot(merged.astype(BF16), wo_ref[...])


def _merge(o_f, o_b, gr, o_d, ga, gb, x, gt1, gla_g, diff_g, wg, wd, wo, lambda_init, tm):
    bsz, s, d = x.shape
    tile = pl.BlockSpec((1, tm, d), lambda b, i: (b, i, 0))
    vec = pl.BlockSpec((1, 1, d), lambda b, i: (b, 0, 0))
    return pl.pallas_call(
        functools.partial(_merge_kernel, lambda_init=lambda_init),
        grid=(bsz, s // tm),
        in_specs=[tile] * 7 + [vec, _resident((1, d)), _resident((1, d)),
                               _resident((d, d)), _resident((d, d)), _resident((d, d))],
        out_specs=tile,
        out_shape=jax.ShapeDtypeStruct((bsz, s, d), F32),
        compiler_params=_cparams("arbitrary", "arbitrary"),
        name="merge",
    )(o_f, o_b, gr, o_d, ga, gb, x, gt1, gla_g.reshape(1, d), diff_g.reshape(1, d), wg, wd, wo)


_CAND_ROW_COUNT = (16, 8, 5, 4, 3, 2, 2, 2)
_SUB = 8
_SENTINEL = 2.0 ** 100


def _top16(a):
    work = a
    vals = []
    for r in range(PEER_TOPK):
        m = jnp.max(work, axis=0, keepdims=True)
        vals.append(m)
        work = jnp.where(work == m, -_SENTINEL * (1.0 + r / 32.0), work)
    rank = jnp.where(work <= -_SENTINEL, (work * (-1.0 / _SENTINEL) - 1.0) * 32.0, float(PEER_TOPK))
    return jnp.concatenate(vals, axis=0), rank


def _route_head(a, b):
    t = a.shape[1]
    top_a, rank_a = _top16(a)
    top_b, rank_b = _top16(b)
    rowi = lax.broadcasted_iota(jnp.int32, (_SUB, t), 0)
    pieces = [top_b + top_a[0:1]]
    for r in range(1, _SUB):
        piece = top_b[0:_SUB] + top_a[r:r + 1]
        n = _CAND_ROW_COUNT[r]
        pieces.append(piece if n == _SUB else jnp.where(rowi < n, piece, -jnp.inf))
    pieces.append(top_a[_SUB:PEER_TOPK] + top_b[0:1])
    cand = jnp.concatenate(pieces, axis=0)
    work = cand
    thr = None
    for _ in range(PEER_TOPK):
        thr = jnp.max(work, axis=0, keepdims=True)
        work = jnp.where(work == thr, -jnp.inf, work)
    sel = cand >= thr
    smax = top_a[0:1] + top_b[0:1]
    z = jnp.sum(jnp.where(sel, jnp.exp(cand - smax), 0.0), axis=0, keepdims=True)
    self32 = sel.astype(F32)
    counts = [jnp.sum(self32[0:PEER_TOPK], axis=0, keepdims=True)]
    for r in range(1, _SUB):
        r0 = PEER_TOPK + (r - 1) * _SUB
        counts.append(jnp.sum(self32[r0:r0 + _SUB], axis=0, keepdims=True))
    tail0 = PEER_TOPK + (_SUB - 1) * _SUB
    for r in range(_SUB, PEER_TOPK):
        counts.append(self32[tail0 + r - _SUB:tail0 + r - _SUB + 1])
    cnt = jnp.zeros(a.shape, F32)
    for r in range(PEER_TOPK):
        cnt = jnp.where(rank_a == float(r), counts[r], cnt)
    wa = jnp.exp(a - top_a[0:1]) * (0.5 / z)
    wb = jnp.exp(b - top_b[0:1])
    return rank_b, wb, cnt, wa


def _route_kernel(x_ref, sh_ref, sc_ref, g_ref, wqt_ref, sk_ref, h2t_ref, rb_ref, wb_ref, cnt_ref, wa_ref):
    h2 = _rms(x_ref[0], g_ref[...]) * (1.0 + sc_ref[0]) + sh_ref[0]
    hb = h2.astype(BF16)
    h2t_ref[0] = h2.T.astype(BF16)
    qt = _dot_nt(wqt_ref[...], hb).astype(BF16)
    for h in range(PEER_HEADS):
        g0 = 2 * h * PEER_NKEYS
        a = _dot(sk_ref[2 * h], qt[g0:g0 + PEER_NKEYS])
        b = _dot(sk_ref[2 * h + 1], qt[g0 + PEER_NKEYS:g0 + 2 * PEER_NKEYS])
        rank_b, wb, cnt, wa = _route_head(a, b)
        rb_ref[0, h] = rank_b.astype(BF16)
        wb_ref[0, h] = wb.astype(BF16)
        cnt_ref[0, h] = cnt
        wa_ref[0, h] = wa


def _route(x1, sh2, sc2, g, wqt, sk, tm):
    bsz, s, d = x1.shape
    vec = pl.BlockSpec((1, 1, d), lambda b, i: (b, 0, 0))
    gate = pl.BlockSpec((1, PEER_HEADS, PEER_NKEYS, tm), lambda b, i: (b, 0, 0, i))
    gshape = lambda dt: jax.ShapeDtypeStruct((bsz, PEER_HEADS, PEER_NKEYS, s), dt)
    return pl.pallas_call(
        _route_kernel,
        grid=(bsz, s // tm),
        in_specs=[pl.BlockSpec((1, tm, d), lambda b, i: (b, i, 0)), vec, vec, _resident((1, d)),
                  _resident(wqt.shape), _resident(sk.shape)],
        out_specs=[pl.BlockSpec((1, d, tm), lambda b, i: (b, 0, i)), gate, gate, gate, gate],
        out_shape=[jax.ShapeDtypeStruct((bsz, d, s), BF16), gshape(BF16), gshape(BF16), gshape(F32), gshape(F32)],
        compiler_params=_cparams("arbitrary", "arbitrary"),
        name="route",
    )(x1, sh2, sc2, g, wqt, sk)


def _dense_kernel(h2t_ref, u_ref, vt_ref, rb_ref, wb_ref, cnt_ref, wa_ref, x_ref, gt_ref, fg_ref, fsh_ref, fsc_ref,
                  o_ref, acc_ref, *, eblk):
    e = pl.program_id(2)

    @pl.when(e == 0)
    def _():
        acc_ref[...] = jnp.zeros_like(acc_ref)

    at = _dot(u_ref[...], h2t_ref[0])
    act = at * (1.0 + lax.erf(at * math.sqrt(0.5)))
    gates = []
    for ii in range(eblk // PEER_NKEYS):
        i = e * (eblk // PEER_NKEYS) + ii
        w = None
        for h in range(PEER_HEADS):
            cnt_i = cnt_ref[0, h, pl.ds(i, 1), :].astype(BF16)
            wa_i = wa_ref[0, h, pl.ds(i, 1), :].astype(BF16)
            wh = jnp.where(rb_ref[0, h] < cnt_i, wb_ref[0, h], jnp.zeros((), BF16)) * wa_i
            w = wh if w is None else w + wh
        gates.append(w)
    gate = jnp.concatenate(gates, axis=0)
    p = act.astype(BF16) * gate
    acc_ref[...] += _dot(vt_ref[0], p)

    @pl.when(e == pl.num_programs(2) - 1)
    def _():
        y = x_ref[0] + gt_ref[0] * acc_ref[...].T
        o_ref[0] = _rms(y, fg_ref[...]) * (1.0 + fsc_ref[0]) + fsh_ref[0]


def _dense(h2t, u, vt, rb, wb, cnt, wa, x1, gt2, fg, fsh, fsc, tm, eblk):
    bsz, s, d = x1.shape
    ne = u.shape[0]
    gate = pl.BlockSpec((1, PEER_HEADS, PEER_NKEYS, tm), lambda b, i, e: (b, 0, 0, i))
    tile = pl.BlockSpec((1, tm, d), lambda b, i, e: (b, i, 0))
    vec = pl.BlockSpec((1, 1, d), lambda b, i, e: (b, 0, 0))
    return pl.pallas_call(
        functools.partial(_dense_kernel, eblk=eblk),
        grid=(bsz, s // tm, ne // eblk),
        in_specs=[pl.BlockSpec((1, d, tm), lambda b, i, e: (b, 0, i)),
                  pl.BlockSpec((eblk, d), lambda b, i, e: (e, 0)),
                  pl.BlockSpec((1, d, eblk), lambda b, i, e: (e, 0, 0)),
                  gate, gate, gate, gate, tile, vec, _resident((1, d)), vec, vec],
        out_specs=tile,
        out_shape=jax.ShapeDtypeStruct((bsz, s, d), F32),
        scratch_shapes=[pltpu.VMEM((d, tm), F32)],
        compiler_params=_cparams("arbitrary", "arbitrary", "arbitrary"),
        name="dense",
    )(h2t, u, vt, rb, wb, cnt, wa, x1, gt2, fg.reshape(1, d), fsh, fsc)


def _tile(s, want):
    return want if s % want == 0 else s


def kernel(x, c, positions, w_ada, b_ada, norm1_g, w_in, gla_wa_fw, gla_ba_fw, gla_wa_bw, gla_ba_bw, gla_norm_g,
           diff_lq1, diff_lk1, diff_lq2, diff_lk2, diff_norm_g, w_gla_proj, w_diff_proj, w_out, norm2_g, peer_wq,
           peer_subkeys, peer_u, peer_v, w_final_ada, b_final_ada, normf_g):
    bsz, s, d = x.shape
    depth = w_ada.shape[0]
    tm = _tile(s, 512)

    half = DIFF_HD // 2
    inv = (ROPE_THETA ** (-np.arange(0, DIFF_HD, 2, dtype=np.float32) / DIFF_HD)).astype(np.float32)
    inv_row = jnp.asarray(np.tile(inv, LANES // half).reshape(1, LANES))
    inv_col = jnp.asarray(inv.reshape(half, 1))

    fmod = _ada(c, w_final_ada, b_final_ada)
    f_shift = fmod[:, :d].reshape(bsz, 1, d)
    f_scale = fmod[:, d:].reshape(bsz, 1, d)

    qk, gv_w, lr_w = 2 * GLA_HEADS * GLA_DK, GLA_HEADS * GLA_DV, 2 * GLA_LOWRANK
    o_gq, o_gk, o_gv, o_gr = 0, qk // 2, qk, qk + gv_w
    o_lr = o_gr + gv_w
    o_dq = o_lr + lr_w
    o_dk, o_dv, o_ga, o_gb = o_dq + d, o_dq + 2 * d, o_dq + 3 * d, o_dq + 4 * d

    for l in range(depth):
        lambda_init = 0.8 - 0.6 * math.exp(-0.3 * l)
        mod = _ada(c, w_ada[l], b_ada[l]).reshape(bsz, 1, N_ADA * d)
        sh1, sc1, gt1, sh2, sc2, gt2 = (mod[:, :, i * d:(i + 1) * d] for i in range(N_ADA))

        w = w_in[l]
        col = lambda c0, n: w[:, c0:c0 + n]
        wn = jnp.concatenate([col(o_gq, qk // 2), col(o_gk, qk // 2), col(o_gv, gv_w), col(o_gr, gv_w),
                              col(o_dk, d), col(o_ga, d), col(o_gb, d)], axis=1).astype(BF16)
        wlr = jnp.zeros((d, LANES), F32).at[:, :lr_w].set(col(o_lr, lr_w)).astype(BF16)
        wt = jnp.concatenate([col(o_dq, d), col(o_dv, d), col(o_gv, gv_w)], axis=1).T.astype(BF16)
        (gq, gk, gv, gr, lr, dk1, dk2, ga, gb, dqt, dvt, gvt) = _inproj(
            x, sh1, sc1, norm1_g[l].reshape(1, d), positions, wn, wlr, wt, inv_row, inv_col, tm)

        o_f, o_b = _gla(gq, gk, gv, gvt, lr, gla_wa_fw[l], gla_ba_fw[l], gla_wa_bw[l], gla_ba_bw[l], _tile(s, 512))
        o_d = _attn(dqt, dk1, dk2, dvt, diff_lq1[l], diff_lk1[l], diff_lq2[l], diff_lk2[l], lambda_init,
                    _tile(s, 512), _tile(s, 256))
        x1 = _merge(o_f, o_b, gr, o_d, ga, gb, x, gt1, gla_norm_g[l], diff_norm_g[l],
                    w_gla_proj[l].astype(BF16), w_diff_proj[l].astype(BF16), w_out[l].astype(BF16), lambda_init, tm)

        sk = peer_subkeys[l].reshape(2 * PEER_HEADS, PEER_NKEYS, -1).astype(BF16)
        h2t, rb, wb, cnt, wa = _route(x1, sh2, sc2, norm2_g[l].reshape(1, d), peer_wq[l].T.astype(BF16), sk,
                                      _tile(s, 256))
        last = l == depth - 1
        assert last, "multi-layer stacks need the final norm split out of the expert kernel"
        eblk = 8 * PEER_NKEYS
        vt = peer_v[l].astype(BF16).reshape(-1, eblk, d).transpose(0, 2, 1)
        x = _dense(h2t, peer_u[l].astype(BF16), vt, rb, wb, cnt, wa, x1, gt2,
                   normf_g, f_shift, f_scale, tm, eblk)
    return x
```

```python
import functools
import math

import numpy as np
import jax
import jax.numpy as jnp
from jax import lax
from jax.experimental import pallas as pl
from jax.experimental.pallas import tpu as pltpu

F32 = jnp.float32
BF16 = jnp.bfloat16

RMS_EPS = 1e-6
ROPE_THETA = 10000.0
GLA_HEADS = 4
GLA_DK = 128
GLA_DV = 256
GLA_LOWRANK = 16
GLA_GATE_NORM = 16.0
GLA_CHUNK = 64
GLA_SUB = 256
DIFF_HEADS = 8
DIFF_HD = 64
DIFF_VD = 128
DIFF_VROWS = DIFF_VD + 16
PEER_HEADS = 8
PEER_NKEYS = 128
PEER_TOPK = 16
N_ADA = 6

LANES = 128
VMEM_LIMIT = 48 * 1024 * 1024

NT_DIMS = (((1,), (1,)), ((), ()))


def _cparams(*sem):
    return pltpu.CompilerParams(dimension_semantics=sem, vmem_limit_bytes=VMEM_LIMIT)


def _dot(a, b):
    return jnp.dot(a, b, preferred_element_type=F32)


def _dot_nt(a, b):
    return lax.dot_general(a, b, NT_DIMS, preferred_element_type=F32)


def _rms(x, g):
    ms = jnp.mean(x * x, axis=-1, keepdims=True)
    return x * lax.rsqrt(ms + RMS_EPS) * g


def _resident(shape):
    nd = len(shape)
    return pl.BlockSpec(shape, lambda *_: (0,) * nd)


def _ada_kernel(c_ref, w_ref, b_ref, o_ref):
    c = c_ref[...]
    ca = c * jax.nn.sigmoid(c)
    o_ref[...] = jnp.dot(ca, w_ref[...], preferred_element_type=F32,
                         precision=lax.Precision.HIGHEST) + b_ref[...]


def _ada(c, w, b):
    bsz, d = c.shape
    n = w.shape[1]
    tn = 2048
    return pl.pallas_call(
        _ada_kernel,
        grid=(n // tn,),
        in_specs=[pl.BlockSpec((bsz, d), lambda j: (0, 0)),
                  pl.BlockSpec((d, tn), lambda j: (0, j)),
                  pl.BlockSpec((1, tn), lambda j: (0, j))],
        out_specs=pl.BlockSpec((bsz, tn), lambda j: (0, j)),
        out_shape=jax.ShapeDtypeStruct((bsz, n), F32),
        compiler_params=_cparams("arbitrary"),
        name="ada",
    )(c, w, b.reshape(1, n))


_C_GQ, _C_GK, _C_GV, _C_GR, _C_DK, _C_GA, _C_GB, _C_END = 0, 512, 1024, 2048, 3072, 4096, 5120, 6144
_R_DQ, _R_DV, _R_GV, _R_END = 0, 1024, 2048, 3072


def _inproj_kernel(x_ref, sh_ref, sc_ref, g_ref, pos_ref, post_ref, invr_ref, invc_ref,
                   wn_ref, wlr_ref, wt_ref,
                   gq_ref, gk_ref, gv_ref, gr_ref, lr_ref, dk1_ref, dk2_ref, ga_ref, gb_ref,
                   dqt_ref, dvt_ref, gvt_ref):
    tm = x_ref.shape[1]
    h = _rms(x_ref[0], g_ref[...]) * (1.0 + sc_ref[0]) + sh_ref[0]
    hb = h.astype(BF16)

    def proj(c0, c1):
        return _dot(hb, wn_ref[:, c0:c1])

    gq_ref[0] = (proj(_C_GQ, _C_GK) * (GLA_DK ** -0.5)).astype(BF16)
    gk_ref[0] = proj(_C_GK, _C_GV).astype(BF16)
    gv_ref[0] = proj(_C_GV, _C_GR).astype(BF16)
    gr_ref[0] = proj(_C_GR, _C_DK).astype(BF16)
    ga_ref[0] = proj(_C_GA, _C_GB).astype(BF16)
    gb_ref[0] = proj(_C_GB, _C_END).astype(BF16)
    lr_ref[0] = _dot(hb, wlr_ref[...])

    ang = pos_ref[0] * invr_ref[...]
    cs = jnp.cos(ang)
    sn = jnp.sin(ang)
    lane = lax.broadcasted_iota(jnp.int32, (tm, LANES), 1)
    first = (lane % DIFF_HD) < (DIFF_HD // 2)
    sn = jnp.where(first, -sn, sn)
    y = proj(_C_DK, _C_GA)
    pad = jnp.where(lane == DIFF_HD, 1.0, 0.0)
    for hd in range(DIFF_HEADS):
        yb = y[:, hd * LANES:(hd + 1) * LANES]
        rot = jnp.where(first, pltpu.roll(yb, LANES - DIFF_HD // 2, 1), pltpu.roll(yb, DIFF_HD // 2, 1))
        kr = yb * cs + rot * sn
        dk1_ref[0, :, hd * LANES:(hd + 1) * LANES] = jnp.where(lane < DIFF_HD, kr, pad).astype(BF16)
        dk2_ref[0, :, hd * LANES:(hd + 1) * LANES] = jnp.where(lane < DIFF_HD, pltpu.roll(kr, DIFF_HD, 1),
                                                                pad).astype(BF16)

    angt = invc_ref[...] * post_ref[0]
    ct = jnp.cos(angt)
    st = jnp.sin(angt)
    half = DIFF_HD // 2
    yt = _dot_nt(wt_ref[_R_DQ:_R_DV, :], hb)
    qscale = DIFF_HD ** -0.5 * math.log2(math.e)
    for blk in range(2 * DIFF_HEADS):
        r0 = blk * DIFF_HD
        x1 = yt[r0:r0 + half]
        x2 = yt[r0 + half:r0 + DIFF_HD]
        dqt_ref[0, r0:r0 + half, :] = ((x1 * ct - x2 * st) * qscale).astype(BF16)
        dqt_ref[0, r0 + half:r0 + DIFF_HD, :] = ((x2 * ct + x1 * st) * qscale).astype(BF16)
    vt = _dot_nt(wt_ref[_R_DV:_R_GV, :], hb).astype(BF16)
    ones = jnp.ones((DIFF_VROWS - DIFF_VD, tm), BF16)
    for hd in range(DIFF_HEADS):
        dvt_ref[0, hd * DIFF_VROWS:hd * DIFF_VROWS + DIFF_VD, :] = vt[hd * DIFF_VD:(hd + 1) * DIFF_VD]
        dvt_ref[0, hd * DIFF_VROWS + DIFF_VD:(hd + 1) * DIFF_VROWS, :] = ones
    gvt_ref[0] = _dot_nt(wt_ref[_R_GV:_R_END, :], hb).astype(BF16)


def _inproj(x, sh, sc, g, pos, wn, wlr, wt, inv_row, inv_col, tm):
    bsz, s, d = x.shape
    posf = pos.astype(F32)
    nat = lambda n: pl.BlockSpec((1, tm, n), lambda b, i: (b, i, 0))
    tr = lambda n: pl.BlockSpec((1, n, tm), lambda b, i: (b, 0, i))
    vec = pl.BlockSpec((1, 1, d), lambda b, i: (b, 0, 0))
    shp = lambda *dims: jax.ShapeDtypeStruct(dims, BF16)
    return pl.pallas_call(
        _inproj_kernel,
        grid=(bsz, s // tm),
        in_specs=[nat(d), vec, vec, _resident((1, d)),
                  pl.BlockSpec((1, tm, 1), lambda b, i: (b, i, 0)),
                  pl.BlockSpec((1, 1, tm), lambda b, i: (b, 0, i)),
                  _resident(inv_row.shape), _resident(inv_col.shape),
                  _resident(wn.shape), _resident(wlr.shape), _resident(wt.shape)],
        out_specs=[nat(512), nat(512), nat(1024), nat(1024), nat(LANES), nat(1024), nat(1024), nat(1024), nat(1024),
                   tr(1024), tr(DIFF_HEADS * DIFF_VROWS), tr(1024)],
        out_shape=[shp(bsz, s, 512), shp(bsz, s, 512), shp(bsz, s, 1024), shp(bsz, s, 1024),
                   jax.ShapeDtypeStruct((bsz, s, LANES), F32),
                   shp(bsz, s, 1024), shp(bsz, s, 1024), shp(bsz, s, 1024), shp(bsz, s, 1024),
                   shp(bsz, 1024, s), shp(bsz, DIFF_HEADS * DIFF_VROWS, s), shp(bsz, 1024, s)],
        compiler_params=_cparams("arbitrary", "arbitrary"),
        name="inproj",
    )(x, sh, sc, g, posf.reshape(bsz, s, 1), posf.reshape(bsz, 1, s), inv_row, inv_col, wn, wlr, wt)


def _gla_direction(q_ref, k_ref, v_ref, vt_ref, lr_ref, wh_ref, wl_ref, ba_ref, tri_ref, o_ref, s_ref, reverse):
    blk = q_ref.shape[1]
    lr = lr_ref[0]
    lr_hi = lr.astype(BF16)
    lr_lo = (lr - lr_hi.astype(F32)).astype(BF16)
    z = _dot(lr_hi, wh_ref[...]) + _dot(lr_lo, wh_ref[...]) + _dot(lr_hi, wl_ref[...]) + ba_ref[...]
    logg = -(jnp.maximum(-z, 0.0) + jnp.log1p(jnp.exp(-jnp.abs(z)))) * (1.0 / GLA_GATE_NORM)
    lg_hi = logg.astype(BF16)
    lg_lo = (logg - lg_hi.astype(F32)).astype(BF16)
    yield
    tri = tri_ref[...]
    sub = tri.shape[0]
    q = q_ref[0].astype(F32)
    k = k_ref[0].astype(F32)
    v = v_ref[0]
    b, qe, o_intra = [], [], []
    for r0 in range(0, blk, sub):
        rows = slice(r0, r0 + sub)
        b_s = _dot(tri, lg_hi[rows]) + _dot(tri, lg_lo[rows])
        qe_s = (q[rows] * jnp.exp(b_s)).astype(BF16)
        ke_s = (k[rows] * jnp.exp(-b_s)).astype(BF16)
        att = jnp.where(tri > 0, _dot_nt(qe_s, ke_s), 0.0).astype(BF16)
        b.append(b_s)
        qe.append(qe_s)
        o_intra.append(_dot(att, v[rows]))
        yield
    b = jnp.concatenate(b, axis=0)
    qe = jnp.concatenate(qe, axis=0)
    o_intra = jnp.concatenate(o_intra, axis=0)

    state = s_ref[...]
    nchunk = blk // GLA_CHUNK
    order = range(nchunk - 1, -1, -1) if reverse else range(nchunk)
    for n in order:
        r0 = n * GLA_CHUNK
        edge = r0 if reverse else r0 + GLA_CHUNK - 1
        b_c = b[r0:r0 + GLA_CHUNK]
        b_e = b[edge:edge + 1]
        kend = (k[r0:r0 + GLA_CHUNK] * jnp.exp(b_e - b_c)).astype(BF16)
        o_inter = _dot_nt(qe[r0:r0 + GLA_CHUNK], state.astype(BF16))
        o_ref[0, r0:r0 + GLA_CHUNK, :] = (o_intra[r0:r0 + GLA_CHUNK] + o_inter).astype(BF16)
        state = state * jnp.exp(b_e) + _dot(vt_ref[0, :, r0:r0 + GLA_CHUNK], kend)
        yield
    s_ref[...] = state


def _gla_kernel(qf, kf, vf, vtf, lrf, qb, kb, vb, vtb, lrb,
                wfh, wfl, baf, wbh, wbl, bab, trif, trib,
                of_ref, ob_ref, sf_ref, sb_ref):
    @pl.when(pl.program_id(2) == 0)
    def _():
        sf_ref[...] = jnp.zeros_like(sf_ref)
        sb_ref[...] = jnp.zeros_like(sb_ref)

    live = [_gla_direction(qf, kf, vf, vtf, lrf, wfh, wfl, baf, trif, of_ref, sf_ref, reverse=False),
            _gla_direction(qb, kb, vb, vtb, lrb, wbh, wbl, bab, trib, ob_ref, sb_ref, reverse=True)]
    while live:
        live = [d for d in live if next(d, True) is None]


def _gla(gq, gk, gv, gvt, lr, wa_f, ba_f, wa_b, ba_b, blk):
    bsz, s, _ = gq.shape
    nblk = s // blk
    def pad_rows(w, r0):
        return jnp.zeros((LANES, w.shape[1]), F32).at[r0:r0 + GLA_LOWRANK].set(w)

    def hi_lo(w):
        hi = w.astype(BF16)
        return hi, (w - hi.astype(F32)).astype(BF16)

    wfh, wfl = hi_lo(pad_rows(wa_f, 0))
    wbh, wbl = hi_lo(pad_rows(wa_b, GLA_LOWRANK))
    sub = min(blk, GLA_SUB)
    r = np.arange(sub)
    same = (r[:, None] // GLA_CHUNK) == (r[None, :] // GLA_CHUNK)
    tri_f = jnp.asarray(same & (r[None, :] <= r[:, None]), BF16)
    tri_b = jnp.asarray(same & (r[None, :] >= r[:, None]), BF16)

    fw = lambda b, h, i: (b, i, h)
    bw = lambda b, h, i: (b, nblk - 1 - i, h)
    fwt = lambda b, h, i: (b, h, i)
    bwt = lambda b, h, i: (b, h, nblk - 1 - i)

    def seq_specs(tok, tok_t):
        return [pl.BlockSpec((1, blk, GLA_DK), tok), pl.BlockSpec((1, blk, GLA_DK), tok),
                pl.BlockSpec((1, blk, GLA_DV), tok), pl.BlockSpec((1, GLA_DV, blk), tok_t),
                pl.BlockSpec((1, blk, LANES), lambda b, h, i, t=tok: (t(b, h, i)[0], t(b, h, i)[1], 0))]

    wspec = pl.BlockSpec((LANES, GLA_DK), lambda b, h, i: (0, h))
    bspec = pl.BlockSpec((1, GLA_DK), lambda b, h, i: (0, h))
    out_f = pl.BlockSpec((1, blk, GLA_DV), fw)
    out_b = pl.BlockSpec((1, blk, GLA_DV), bw)
    o_shape = jax.ShapeDtypeStruct((bsz, s, GLA_HEADS * GLA_DV), BF16)
    return pl.pallas_call(
        _gla_kernel,
        grid=(bsz, GLA_HEADS, nblk),
        in_specs=seq_specs(fw, fwt) + seq_specs(bw, bwt) + [wspec, wspec, bspec, wspec, wspec, bspec,
                                                             _resident((sub, sub)), _resident((sub, sub))],
        out_specs=[out_f, out_b],
        out_shape=[o_shape, o_shape],
        scratch_shapes=[pltpu.VMEM((GLA_DV, GLA_DK), F32), pltpu.VMEM((GLA_DV, GLA_DK), F32)],
        compiler_params=_cparams("arbitrary", "arbitrary", "arbitrary"),
        name="gla",
    )(gq, gk, gv, gvt, lr, gq, gk, gv, gvt, lr,
      wfh, wfl, ba_f.reshape(1, -1), wbh, wbl, ba_b.reshape(1, -1), tri_f, tri_b)


ATTN_SAFE_SHIFT = 40.0
ATTN_NORM_CHUNK = 1024


def _ring(nk, nslot, stages):
    assert nk % nslot == 0
    offs = [off for _, off in stages]

    def static_iter(i):
        for fn, off in stages:
            if 0 <= i + off < nk:
                fn(i + off, (i + off) % nslot)

    def full(g):
        return all(0 <= g * nslot + u + off < nk for u in range(nslot) for off in offs)

    ngroup = nk // nslot
    interior = [g for g in range(ngroup) if full(g)]
    g_lo, g_hi = (interior[0], interior[-1] + 1) if interior else (ngroup, ngroup)
    for i in range(-max(offs), g_lo * nslot):
        static_iter(i)
    if g_hi - g_lo == 1:
        for i in range(g_lo * nslot, g_hi * nslot):
            static_iter(i)
    elif g_hi > g_lo:
        def group(g, carry):
            for u in range(nslot):
                for fn, off in stages:
                    fn(g * nslot + u + off, (u + off) % nslot)
            return carry
        lax.fori_loop(g_lo, g_hi, group, 0)
    for i in range(g_hi * nslot, nk - min(offs)):
        static_iter(i)


def _attn_kernel(lq1_ref, lk1_ref, lq2_ref, lk2_ref, qt_ref, k1_ref, k2_ref, vt_ref, o_ref,
                 s_ref, p_ref, a_ref, mt_ref, m_ref, acc_ref, kmax_ref, *, tk, lambda_init):
    k_refs = (k1_ref, k2_ref)
    nkeys = k1_ref.shape[1]
    nk = nkeys // tk
    nslot = p_ref.shape[0]
    dist = nslot // 2
    tq = qt_ref.shape[2]
    nchunk = max(nkeys // ATTN_NORM_CHUNK, 1)
    kc = nkeys // nchunk

    @pl.when(pl.program_id(2) == 0)
    def _():
        lane = lax.broadcasted_iota(jnp.int32, (kc, LANES), 1)
        for mi in range(2):
            def chunk(i, mx):
                kk = k_refs[mi][0, pl.ds(pl.multiple_of(i * kc, kc), kc), :].astype(F32)
                kk = jnp.where(lane < DIFF_HD, kk, 0.0)
                n2 = jnp.sum(kk * kk, axis=1, keepdims=True)
                return jnp.maximum(mx, jnp.max(n2, axis=0, keepdims=True))
            kmax_ref[mi] = jnp.sqrt(lax.fori_loop(0, nchunk, chunk, jnp.zeros((1, 1), F32)))

    qt = qt_ref[0]
    row = lax.broadcasted_iota(jnp.int32, (16, tq), 0)
    q_shift, q_plain, bound = [], [], []
    for mi in range(2):
        qm = qt[mi * DIFF_HD:(mi + 1) * DIFF_HD]
        qf = qm.astype(F32)
        c = jnp.sqrt(jnp.sum(qf * qf, axis=0, keepdims=True)) * kmax_ref[mi]
        c = c.astype(BF16).astype(F32)
        shift_rows = jnp.where(row == 0, -c, 0.0).astype(BF16)
        q_shift.append(jnp.concatenate([qm, shift_rows, jnp.zeros((DIFF_HD - 16, tq), BF16)], axis=0))
        q_plain.append(jnp.concatenate([qm, jnp.zeros((DIFF_HD, tq), BF16)], axis=0))
        bound.append(jnp.max(c))
    safe = jnp.maximum(bound[0], bound[1]) <= ATTN_SAFE_SHIFT
    acc_ref[...] = jnp.zeros_like(acc_ref)

    def key_tile(mi, t):
        return k_refs[mi][0, pl.ds(pl.multiple_of(t * tk, tk), tk), :]

    def value_tile(t):
        return vt_ref[0, :, pl.ds(pl.multiple_of(t * tk, tk), tk)]

    @pl.when(safe)
    def _():
        def probs(t, slot):
            for mi in range(2):
                p_ref[slot, mi] = jnp.exp2(_dot(key_tile(mi, t), q_shift[mi])).astype(BF16)

        def values(t, slot):
            vtj = value_tile(t)
            for mi in range(2):
                acc_ref[mi] += _dot(vtj, p_ref[slot, mi])

        _ring(nk, nslot, [(values, 0), (probs, dist)])

    @pl.when(jnp.logical_not(safe))
    def _():
        m_ref[...] = jnp.full(m_ref.shape, -jnp.inf, F32)

        def scores(t, slot):
            for mi in range(2):
                s = _dot(key_tile(mi, t), q_plain[mi])
                s_ref[slot, mi] = s
                mt_ref[slot, mi] = jnp.max(s, axis=0, keepdims=True)

        def softmax(t, slot):
            for mi in range(2):
                m_old = m_ref[mi]
                m_new = jnp.maximum(m_old, mt_ref[slot, mi])
                a_ref[slot, mi] = jnp.exp2(m_old - m_new)
                p_ref[slot, mi] = jnp.exp2(s_ref[slot, mi] - m_new).astype(BF16)
                m_ref[mi] = m_new

        def values(t, slot):
            vtj = value_tile(t)
            for mi in range(2):
                acc_ref[mi] = acc_ref[mi] * a_ref[slot, mi] + _dot(vtj, p_ref[slot, mi])

        _ring(nk, nslot, [(softmax, 0), (values, -dist), (scores, dist)])

    lam = (jnp.exp(jnp.sum(lq1_ref[...] * lk1_ref[...], axis=-1, keepdims=True))
           - jnp.exp(jnp.sum(lq2_ref[...] * lk2_ref[...], axis=-1, keepdims=True)) + lambda_init)
    num = [acc_ref[mi, 0:DIFF_VD, :] for mi in range(2)]
    den = [acc_ref[mi, DIFF_VD:DIFF_VD + 1, :] for mi in range(2)]
    ot = num[0] * (1.0 / den[0]) - lam * (num[1] * (1.0 / den[1]))
    o_ref[0] = ot.T.astype(BF16)


def _attn(dqt, dk1, dk2, dvt, lq1, lk1, lq2, lk2, lambda_init, tq, tk):
    bsz, s, _ = dk1.shape
    lspec = _resident((1, DIFF_HD))
    kspec = pl.BlockSpec((1, s, LANES), lambda b, h, i: (b, 0, h))
    return pl.pallas_call(
        functools.partial(_attn_kernel, tk=tk, lambda_init=lambda_init),
        grid=(bsz, DIFF_HEADS, s // tq),
        in_specs=[lspec, lspec, lspec, lspec,
                  pl.BlockSpec((1, 2 * DIFF_HD, tq), lambda b, h, i: (b, h, i)),
                  kspec, kspec,
                  pl.BlockSpec((1, DIFF_VROWS, s), lambda b, h, i: (b, h, 0))],
        out_specs=pl.BlockSpec((1, tq, DIFF_VD), lambda b, h, i: (b, i, h)),
        out_shape=jax.ShapeDtypeStruct((bsz, s, DIFF_HEADS * DIFF_VD), BF16),
        scratch_shapes=[pltpu.VMEM((4, 2, tk, tq), F32), pltpu.VMEM((4, 2, tk, tq), BF16),
                        pltpu.VMEM((4, 2, 1, tq), F32), pltpu.VMEM((4, 2, 1, tq), F32),
                        pltpu.VMEM((2, 1, tq), F32),
                        pltpu.VMEM((2, DIFF_VROWS, tq), F32),
                        pltpu.VMEM((2, 1, 1), F32)],
        compiler_params=_cparams("arbitrary", "arbitrary", "arbitrary"),
        name="attn",
    )(lq1.reshape(1, -1), lk1.reshape(1, -1), lq2.reshape(1, -1), lk2.reshape(1, -1), dqt, dk1, dk2, dvt)


def _merge_kernel(of_ref, ob_ref, gr_ref, od_ref, ga_ref, gb_ref, x_ref, gt_ref, gg_ref, dg_ref,
                  wg_ref, wd_ref, wo_ref, o_ref, *, lambda_init):
    og = of_ref[0].astype(F32) + ob_ref[0].astype(F32)
    gg = gg_ref[...]
    og = jnp.concatenate(
        [_rms(og[:, h * GLA_DV:(h + 1) * GLA_DV], gg[:, h * GLA_DV:(h + 1) * GLA_DV]) for h in range(GLA_HEADS)],
        axis=-1)
    gr = gr_ref[0].astype(F32)
    y_gla = _dot((og * (gr * jax.nn.sigmoid(gr))).astype(BF16), wg_ref[...])
    od = od_ref[0].astype(F32)
    dg = dg_ref[...]
    od = jnp.concatenate(
        [_rms(od[:, h * DIFF_VD:(h + 1) * DIFF_VD], dg[:, h * DIFF_VD:(h + 1) * DIFF_VD]) for h in range(DIFF_HEADS)],
        axis=-1) * (1.0 - lambda_init)
    y_diff = _dot(od.astype(BF16), wd_ref[...])
    merged = (jax.nn.sigmoid(ga_ref[0].astype(F32)) * y_gla + jax.nn.sigmoid(gb_ref[0].astype(F32)) * y_diff)
    o_ref[0] = x_ref[0] + gt_ref[0] * _dot(merged.astype(BF16), wo_ref[...])


def _merge(o_f, o_b, gr, o_d, ga, gb, x, gt1, gla_g, diff_g, wg, wd, wo, lambda_init, tm):
    bsz, s, d = x.shape
    tile = pl.BlockSpec((1, tm, d), lambda b, i: (b, i, 0))
    vec = pl.BlockSpec((1, 1, d), lambda b, i: (b, 0, 0))
    return pl.pallas_call(
        functools.partial(_merge_kernel, lambda_init=lambda_init),
        grid=(bsz, s // tm),
        in_specs=[tile] * 7 + [vec, _resident((1, d)), _resident((1, d)),
                               _resident((d, d)), _resident((d, d)), _resident((d, d))],
        out_specs=tile,
        out_shape=jax.ShapeDtypeStruct((bsz, s, d), F32),
        compiler_params=_cparams("arbitrary", "arbitrary"),
        name="merge",
    )(o_f, o_b, gr, o_d, ga, gb, x, gt1, gla_g.reshape(1, d), diff_g.reshape(1, d), wg, wd, wo)


_CAND_ROW_COUNT = (16, 8, 5, 4, 3, 2, 2, 2)
_SUB = 8
_SENTINEL = 2.0 ** 100


def _top16(a):
    work = a
    vals = []
    for r in range(PEER_TOPK):
        m = jnp.max(work, axis=0, keepdims=True)
        vals.append(m)
        work = jnp.where(work == m, -_SENTINEL * (1.0 + r / 32.0), work)
    rank = jnp.where(work <= -_SENTINEL, (work * (-1.0 / _SENTINEL) - 1.0) * 32.0, float(PEER_TOPK))
    return jnp.concatenate(vals, axis=0), rank


def _route_head(a, b):
    t = a.shape[1]
    top_a, rank_a = _top16(a)
    top_b, rank_b = _top16(b)
    rowi = lax.broadcasted_iota(jnp.int32, (_SUB, t), 0)
    pieces = [top_b + top_a[0:1]]
    for r in range(1, _SUB):
        piece = top_b[0:_SUB] + top_a[r:r + 1]
        n = _CAND_ROW_COUNT[r]
        pieces.append(piece if n == _SUB else jnp.where(rowi < n, piece, -jnp.inf))
    pieces.append(top_a[_SUB:PEER_TOPK] + top_b[0:1])
    cand = jnp.concatenate(pieces, axis=0)
    work = cand
    thr = None
    for _ in range(PEER_TOPK):
        thr = jnp.max(work, axis=0, keepdims=True)
        work = jnp.where(work == thr, -jnp.inf, work)
    sel = cand >= thr
    smax = top_a[0:1] + top_b[0:1]
    z = jnp.sum(jnp.where(sel, jnp.exp(cand - smax), 0.0), axis=0, keepdims=True)
    self32 = sel.astype(F32)
    counts = [jnp.sum(self32[0:PEER_TOPK], axis=0, keepdims=True)]
    for r in range(1, _SUB):
        r0 = PEER_TOPK + (r - 1) * _SUB
        counts.append(jnp.sum(self32[r0:r0 + _SUB], axis=0, keepdims=True))
    tail0 = PEER_TOPK + (_SUB - 1) * _SUB
    for r in range(_SUB, PEER_TOPK):
        counts.append(self32[tail0 + r - _SUB:tail0 + r - _SUB + 1])
    cnt = jnp.zeros(a.shape, F32)
    for r in range(PEER_TOPK):
        cnt = jnp.where(rank_a == float(r), counts[r], cnt)
    wa = jnp.exp(a - top_a[0:1]) * (0.5 / z)
    wb = jnp.exp(b - top_b[0:1])
    return rank_b, wb, cnt, wa


def _route_kernel(x_ref, sh_ref, sc_ref, g_ref, wqt_ref, sk_ref, h2t_ref, rb_ref, wb_ref, cnt_ref, wa_ref):
    h2 = _rms(x_ref[0], g_ref[...]) * (1.0 + sc_ref[0]) + sh_ref[0]
    hb = h2.astype(BF16)
    h2t_ref[0] = h2.T.astype(BF16)
    qt = _dot_nt(wqt_ref[...], hb).astype(BF16)
    for h in range(PEER_HEADS):
        g0 = 2 * h * PEER_NKEYS
        a = _dot(sk_ref[2 * h], qt[g0:g0 + PEER_NKEYS])
        b = _dot(sk_ref[2 * h + 1], qt[g0 + PEER_NKEYS:g0 + 2 * PEER_NKEYS])
        rank_b, wb, cnt, wa = _route_head(a, b)
        rb_ref[0, h] = rank_b.astype(BF16)
        wb_ref[0, h] = wb.astype(BF16)
        cnt_ref[0, h] = cnt
        wa_ref[0, h] = wa


def _route(x1, sh2, sc2, g, wqt, sk, tm):
    bsz, s, d = x1.shape
    vec = pl.BlockSpec((1, 1, d), lambda b, i: (b, 0, 0))
    gate = pl.BlockSpec((1, PEER_HEADS, PEER_NKEYS, tm), lambda b, i: (b, 0, 0, i))
    gshape = lambda dt: jax.ShapeDtypeStruct((bsz, PEER_HEADS, PEER_NKEYS, s), dt)
    return pl.pallas_call(
        _route_kernel,
        grid=(bsz, s // tm),
        in_specs=[pl.BlockSpec((1, tm, d), lambda b, i: (b, i, 0)), vec, vec, _resident((1, d)),
                  _resident(wqt.shape), _resident(sk.shape)],
        out_specs=[pl.BlockSpec((1, d, tm), lambda b, i: (b, 0, i)), gate, gate, gate, gate],
        out_shape=[jax.ShapeDtypeStruct((bsz, d, s), BF16), gshape(BF16), gshape(BF16), gshape(F32), gshape(F32)],
        compiler_params=_cparams("arbitrary", "arbitrary"),
        name="route",
    )(x1, sh2, sc2, g, wqt, sk)


def _dense_kernel(h2t_ref, u_ref, vt_ref, rb_ref, wb_ref, cnt_ref, wa_ref, x_ref, gt_ref, fg_ref, fsh_ref, fsc_ref,
                  o_ref, acc_ref, *, eblk):
    e = pl.program_id(2)

    @pl.when(e == 0)
    def _():
        acc_ref[...] = jnp.zeros_like(acc_ref)

    at = _dot(u_ref[...], h2t_ref[0])
    act = at * (1.0 + lax.erf(at * math.sqrt(0.5)))
    gates = []
    for ii in range(eblk // PEER_NKEYS):
        w = None
        for h in range(PEER_HEADS):
            cnt_i = cnt_ref[0, h, 0, ii:ii + 1, :].astype(BF16)
            wa_i = wa_ref[0, h, 0, ii:ii + 1, :].astype(BF16)
            wh = jnp.where(rb_ref[0, h] < cnt_i, wb_ref[0, h], jnp.zeros((), BF16)) * wa_i
            w = wh if w is None else w + wh
        gates.append(w)
    gate = jnp.concatenate(gates, axis=0)
    acc_ref[...] += _dot(vt_ref[0], act.astype(BF16) * gate)

    @pl.when(e == pl.num_programs(2) - 1)
    def _():
        y = x_ref[0] + gt_ref[0] * acc_ref[...].T
        o_ref[0] = _rms(y, fg_ref[...]) * (1.0 + fsc_ref[0]) + fsh_ref[0]


def _dense(h2t, u, vt, rb, wb, cnt, wa, x1, gt2, fg, fsh, fsc, tm, eblk):
    bsz, s, d = x1.shape
    nblk = u.shape[0] // eblk
    gate = pl.BlockSpec((1, PEER_HEADS, PEER_NKEYS, tm), lambda b, i, e: (b, 0, 0, i))
    nrow = eblk // PEER_NKEYS
    rows = pl.BlockSpec((1, PEER_HEADS, 1, nrow, tm), lambda b, i, e: (b, 0, e, 0, i))
    by_block = lambda a: a.reshape(bsz, PEER_HEADS, nblk, nrow, s)
    tile = pl.BlockSpec((1, tm, d), lambda b, i, e: (b, i, 0))
    vec = pl.BlockSpec((1, 1, d), lambda b, i, e: (b, 0, 0))
    return pl.pallas_call(
        functools.partial(_dense_kernel, eblk=eblk),
        grid=(bsz, s // tm, nblk),
        in_specs=[pl.BlockSpec((1, d, tm), lambda b, i, e: (b, 0, i)),
                  pl.BlockSpec((eblk, d), lambda b, i, e: (e, 0)),
                  pl.BlockSpec((1, d, eblk), lambda b, i, e: (e, 0, 0)),
                  gate, gate, rows, rows, tile, vec, _resident((1, d)), vec, vec],
        out_specs=tile,
        out_shape=jax.ShapeDtypeStruct((bsz, s, d), F32),
        scratch_shapes=[pltpu.VMEM((d, tm), F32)],
        compiler_params=_cparams("arbitrary", "arbitrary", "arbitrary"),
        name="dense",
    )(h2t, u, vt, rb, wb, by_block(cnt), by_block(wa), x1, gt2, fg.reshape(1, d), fsh, fsc)


def _tile(s, want):
    return want if s % want == 0 else s


def kernel(x, c, positions, w_ada, b_ada, norm1_g, w_in, gla_wa_fw, gla_ba_fw, gla_wa_bw, gla_ba_bw, gla_norm_g,
           diff_lq1, diff_lk1, diff_lq2, diff_lk2, diff_norm_g, w_gla_proj, w_diff_proj, w_out, norm2_g, peer_wq,
           peer_subkeys, peer_u, peer_v, w_final_ada, b_final_ada, normf_g):
    bsz, s, d = x.shape
    depth = w_ada.shape[0]
    tm = _tile(s, 512)

    half = DIFF_HD // 2
    inv = (ROPE_THETA ** (-np.arange(0, DIFF_HD, 2, dtype=np.float32) / DIFF_HD)).astype(np.float32)
    inv_row = jnp.asarray(np.tile(inv, LANES // half).reshape(1, LANES))
    inv_col = jnp.asarray(inv.reshape(half, 1))

    fmod = _ada(c, w_final_ada, b_final_ada)
    f_shift = fmod[:, :d].reshape(bsz, 1, d)
    f_scale = fmod[:, d:].reshape(bsz, 1, d)

    qk, gv_w, lr_w = 2 * GLA_HEADS * GLA_DK, GLA_HEADS * GLA_DV, 2 * GLA_LOWRANK
    o_gq, o_gk, o_gv, o_gr = 0, qk // 2, qk, qk + gv_w
    o_lr = o_gr + gv_w
    o_dq = o_lr + lr_w
    o_dk, o_dv, o_ga, o_gb = o_dq + d, o_dq + 2 * d, o_dq + 3 * d, o_dq + 4 * d

    for l in range(depth):
        lambda_init = 0.8 - 0.6 * math.exp(-0.3 * l)
        mod = _ada(c, w_ada[l], b_ada[l]).reshape(bsz, 1, N_ADA * d)
        sh1, sc1, gt1, sh2, sc2, gt2 = (mod[:, :, i * d:(i + 1) * d] for i in range(N_ADA))

        w = w_in[l]
        col = lambda c0, n: w[:, c0:c0 + n]
        wn = jnp.concatenate([col(o_gq, qk // 2), col(o_gk, qk // 2), col(o_gv, gv_w), col(o_gr, gv_w),
                              col(o_dk, d), col(o_ga, d), col(o_gb, d)], axis=1).astype(BF16)
        wlr = jnp.zeros((d, LANES), F32).at[:, :lr_w].set(col(o_lr, lr_w)).astype(BF16)
        wt = jnp.concatenate([col(o_dq, d), col(o_dv, d), col(o_gv, gv_w)], axis=1).T.astype(BF16)
        (gq, gk, gv, gr, lr, dk1, dk2, ga, gb, dqt, dvt, gvt) = _inproj(
            x, sh1, sc1, norm1_g[l].reshape(1, d), positions, wn, wlr, wt, inv_row, inv_col, tm)

        o_f, o_b = _gla(gq, gk, gv, gvt, lr, gla_wa_fw[l], gla_ba_fw[l], gla_wa_bw[l], gla_ba_bw[l], _tile(s, 512))
        o_d = _attn(dqt, dk1, dk2, dvt, diff_lq1[l], diff_lk1[l], diff_lq2[l], diff_lk2[l], lambda_init,
                    _tile(s, 512), _tile(s, 256))
        x1 = _merge(o_f, o_b, gr, o_d, ga, gb, x, gt1, gla_norm_g[l], diff_norm_g[l],
                    w_gla_proj[l].astype(BF16), w_diff_proj[l].astype(BF16), w_out[l].astype(BF16), lambda_init, tm)

        sk = peer_subkeys[l].reshape(2 * PEER_HEADS, PEER_NKEYS, -1).astype(BF16)
        h2t, rb, wb, cnt, wa = _route(x1, sh2, sc2, norm2_g[l].reshape(1, d), peer_wq[l].T.astype(BF16), sk,
                                      _tile(s, 256))
        last = l == depth - 1
        assert last, "multi-layer stacks need the final norm split out of the expert kernel"
        eblk = 8 * PEER_NKEYS
        vt = peer_v[l].astype(BF16).reshape(-1, eblk, d).transpose(0, 2, 1)
        x = _dense(h2t, peer_u[l].astype(BF16), vt, rb, wb, cnt, wa, x1, gt2,
                   normf_g, f_shift, f_scale, tm, eblk)
    return x
```

```python
import functools
import math

import numpy as np
import jax
import jax.numpy as jnp
from jax import lax
from jax.experimental import pallas as pl
from jax.experimental.pallas import tpu as pltpu

F32 = jnp.float32
BF16 = jnp.bfloat16

RMS_EPS = 1e-6
ROPE_THETA = 10000.0
GLA_HEADS = 4
GLA_DK = 128
GLA_DV = 256
GLA_LOWRANK = 16
GLA_GATE_NORM = 16.0
GLA_CHUNK = 64
GLA_SUB = 256
DIFF_HEADS = 8
DIFF_HD = 64
DIFF_VD = 128
PEER_HEADS = 8
PEER_NKEYS = 128
PEER_TOPK = 16
N_ADA = 6

LANES = 128
VMEM_LIMIT = 48 * 1024 * 1024

NT_DIMS = (((1,), (1,)), ((), ()))


def _cparams(*sem):
    return pltpu.CompilerParams(dimension_semantics=sem, vmem_limit_bytes=VMEM_LIMIT)


def _dot(a, b):
    return jnp.dot(a, b, preferred_element_type=F32)


def _dot_nt(a, b):
    return lax.dot_general(a, b, NT_DIMS, preferred_element_type=F32)


def _rms(x, g):
    ms = jnp.mean(x * x, axis=-1, keepdims=True)
    return x * lax.rsqrt(ms + RMS_EPS) * g


def _resident(shape):
    nd = len(shape)
    return pl.BlockSpec(shape, lambda *_: (0,) * nd)


def _ada_kernel(c_ref, w_ref, b_ref, o_ref):
    c = c_ref[...]
    ca = c * jax.nn.sigmoid(c)
    o_ref[...] = jnp.dot(ca, w_ref[...], preferred_element_type=F32,
                         precision=lax.Precision.HIGHEST) + b_ref[...]


def _ada(c, w, b):
    bsz, d = c.shape
    n = w.shape[1]
    tn = 2048
    return pl.pallas_call(
        _ada_kernel,
        grid=(n // tn,),
        in_specs=[pl.BlockSpec((bsz, d), lambda j: (0, 0)),
                  pl.BlockSpec((d, tn), lambda j: (0, j)),
                  pl.BlockSpec((1, tn), lambda j: (0, j))],
        out_specs=pl.BlockSpec((bsz, tn), lambda j: (0, j)),
        out_shape=jax.ShapeDtypeStruct((bsz, n), F32),
        compiler_params=_cparams("arbitrary"),
        name="ada",
    )(c, w, b.reshape(1, n))


_C_GQ, _C_GK, _C_GV, _C_GR, _C_DK, _C_GA, _C_GB, _C_END = 0, 512, 1024, 2048, 3072, 4096, 5120, 6144
_R_DQ, _R_DV, _R_GV, _R_END = 0, 1024, 2048, 3072


def _inproj_kernel(x_ref, sh_ref, sc_ref, g_ref, pos_ref, post_ref, invr_ref, invc_ref,
                   wn_ref, wlr_ref, wt_ref,
                   gq_ref, gk_ref, gv_ref, gr_ref, lr_ref, dk1_ref, dk2_ref, ga_ref, gb_ref,
                   dqt_ref, dvt_ref, gvt_ref):
    tm = x_ref.shape[1]
    h = _rms(x_ref[0], g_ref[...]) * (1.0 + sc_ref[0]) + sh_ref[0]
    hb = h.astype(BF16)

    def proj(c0, c1):
        return _dot(hb, wn_ref[:, c0:c1])

    gq_ref[0] = (proj(_C_GQ, _C_GK) * (GLA_DK ** -0.5)).astype(BF16)
    gk_ref[0] = proj(_C_GK, _C_GV).astype(BF16)
    gv_ref[0] = proj(_C_GV, _C_GR).astype(BF16)
    gr_ref[0] = proj(_C_GR, _C_DK).astype(BF16)
    ga_ref[0] = proj(_C_GA, _C_GB).astype(BF16)
    gb_ref[0] = proj(_C_GB, _C_END).astype(BF16)
    lr_ref[0] = _dot(hb, wlr_ref[...])

    ang = pos_ref[0] * invr_ref[...]
    cs = jnp.cos(ang)
    sn = jnp.sin(ang)
    lane = lax.broadcasted_iota(jnp.int32, (tm, LANES), 1)
    first = (lane % DIFF_HD) < (DIFF_HD // 2)
    sn = jnp.where(first, -sn, sn)
    y = proj(_C_DK, _C_GA)
    pad = jnp.where(lane == DIFF_HD, 1.0, 0.0)
    for hd in range(DIFF_HEADS):
        yb = y[:, hd * LANES:(hd + 1) * LANES]
        rot = jnp.where(first, pltpu.roll(yb, LANES - DIFF_HD // 2, 1), pltpu.roll(yb, DIFF_HD // 2, 1))
        kr = yb * cs + rot * sn
        dk1_ref[0, :, hd * LANES:(hd + 1) * LANES] = jnp.where(lane < DIFF_HD, kr, pad).astype(BF16)
        dk2_ref[0, :, hd * LANES:(hd + 1) * LANES] = jnp.where(lane < DIFF_HD, pltpu.roll(kr, DIFF_HD, 1),
                                                                pad).astype(BF16)

    angt = invc_ref[...] * post_ref[0]
    ct = jnp.cos(angt)
    st = jnp.sin(angt)
    half = DIFF_HD // 2
    yt = _dot_nt(wt_ref[_R_DQ:_R_DV, :], hb)
    qscale = DIFF_HD ** -0.5 * math.log2(math.e)
    for blk in range(2 * DIFF_HEADS):
        r0 = blk * DIFF_HD
        x1 = yt[r0:r0 + half]
        x2 = yt[r0 + half:r0 + DIFF_HD]
        dqt_ref[0, r0:r0 + half, :] = ((x1 * ct - x2 * st) * qscale).astype(BF16)
        dqt_ref[0, r0 + half:r0 + DIFF_HD, :] = ((x2 * ct + x1 * st) * qscale).astype(BF16)
    dvt_ref[0] = _dot_nt(wt_ref[_R_DV:_R_GV, :], hb).astype(BF16)
    gvt_ref[0] = _dot_nt(wt_ref[_R_GV:_R_END, :], hb).astype(BF16)


def _inproj(x, sh, sc, g, pos, wn, wlr, wt, inv_row, inv_col, tm):
    bsz, s, d = x.shape
    posf = pos.astype(F32)
    nat = lambda n: pl.BlockSpec((1, tm, n), lambda b, i: (b, i, 0))
    tr = lambda n: pl.BlockSpec((1, n, tm), lambda b, i: (b, 0, i))
    vec = pl.BlockSpec((1, 1, d), lambda b, i: (b, 0, 0))
    shp = lambda *dims: jax.ShapeDtypeStruct(dims, BF16)
    return pl.pallas_call(
        _inproj_kernel,
        grid=(bsz, s // tm),
        in_specs=[nat(d), vec, vec, _resident((1, d)),
                  pl.BlockSpec((1, tm, 1), lambda b, i: (b, i, 0)),
                  pl.BlockSpec((1, 1, tm), lambda b, i: (b, 0, i)),
                  _resident(inv_row.shape), _resident(inv_col.shape),
                  _resident(wn.shape), _resident(wlr.shape), _resident(wt.shape)],
        out_specs=[nat(512), nat(512), nat(1024), nat(1024), nat(LANES), nat(1024), nat(1024), nat(1024), nat(1024),
                   tr(1024), tr(1024), tr(1024)],
        out_shape=[shp(bsz, s, 512), shp(bsz, s, 512), shp(bsz, s, 1024), shp(bsz, s, 1024),
                   jax.ShapeDtypeStruct((bsz, s, LANES), F32),
                   shp(bsz, s, 1024), shp(bsz, s, 1024), shp(bsz, s, 1024), shp(bsz, s, 1024),
                   shp(bsz, 1024, s), shp(bsz, 1024, s), shp(bsz, 1024, s)],
        compiler_params=_cparams("arbitrary", "arbitrary"),
        name="inproj",
    )(x, sh, sc, g, posf.reshape(bsz, s, 1), posf.reshape(bsz, 1, s), inv_row, inv_col, wn, wlr, wt)


def _gla_direction(q_ref, k_ref, v_ref, vt_ref, lr_ref, wh_ref, wl_ref, ba_ref, tri_ref, o_ref, s_ref, reverse):
    blk = q_ref.shape[1]
    lr = lr_ref[0]
    lr_hi = lr.astype(BF16)
    lr_lo = (lr - lr_hi.astype(F32)).astype(BF16)
    z = _dot(lr_hi, wh_ref[...]) + _dot(lr_lo, wh_ref[...]) + _dot(lr_hi, wl_ref[...]) + ba_ref[...]
    logg = -(jnp.maximum(-z, 0.0) + jnp.log1p(jnp.exp(-jnp.abs(z)))) * (1.0 / GLA_GATE_NORM)
    lg_hi = logg.astype(BF16)
    lg_lo = (logg - lg_hi.astype(F32)).astype(BF16)
    yield
    tri = tri_ref[...]
    sub = tri.shape[0]
    q = q_ref[0].astype(F32)
    k = k_ref[0].astype(F32)
    v = v_ref[0]
    b, qe, o_intra = [], [], []
    for r0 in range(0, blk, sub):
        rows = slice(r0, r0 + sub)
        b_s = _dot(tri, lg_hi[rows]) + _dot(tri, lg_lo[rows])
        qe_s = (q[rows] * jnp.exp(b_s)).astype(BF16)
        ke_s = (k[rows] * jnp.exp(-b_s)).astype(BF16)
        att = jnp.where(tri > 0, _dot_nt(qe_s, ke_s), 0.0).astype(BF16)
        b.append(b_s)
        qe.append(qe_s)
        o_intra.append(_dot(att, v[rows]))
        yield
    b = jnp.concatenate(b, axis=0)
    qe = jnp.concatenate(qe, axis=0)
    o_intra = jnp.concatenate(o_intra, axis=0)

    state = s_ref[...]
    nchunk = blk // GLA_CHUNK
    order = range(nchunk - 1, -1, -1) if reverse else range(nchunk)
    for n in order:
        r0 = n * GLA_CHUNK
        edge = r0 if reverse else r0 + GLA_CHUNK - 1
        b_c = b[r0:r0 + GLA_CHUNK]
        b_e = b[edge:edge + 1]
        kend = (k[r0:r0 + GLA_CHUNK] * jnp.exp(b_e - b_c)).astype(BF16)
        o_inter = _dot_nt(qe[r0:r0 + GLA_CHUNK], state.astype(BF16))
        o_ref[0, r0:r0 + GLA_CHUNK, :] = (o_intra[r0:r0 + GLA_CHUNK] + o_inter).astype(BF16)
        state = state * jnp.exp(b_e) + _dot(vt_ref[0, :, r0:r0 + GLA_CHUNK], kend)
        yield
    s_ref[...] = state


def _gla_kernel(qf, kf, vf, vtf, lrf, qb, kb, vb, vtb, lrb,
                wfh, wfl, baf, wbh, wbl, bab, trif, trib,
                of_ref, ob_ref, sf_ref, sb_ref):
    @pl.when(pl.program_id(2) == 0)
    def _():
        sf_ref[...] = jnp.zeros_like(sf_ref)
        sb_ref[...] = jnp.zeros_like(sb_ref)

    live = [_gla_direction(qf, kf, vf, vtf, lrf, wfh, wfl, baf, trif, of_ref, sf_ref, reverse=False),
            _gla_direction(qb, kb, vb, vtb, lrb, wbh, wbl, bab, trib, ob_ref, sb_ref, reverse=True)]
    while live:
        live = [d for d in live if next(d, True) is None]


def _gla(gq, gk, gv, gvt, lr, wa_f, ba_f, wa_b, ba_b, blk):
    bsz, s, _ = gq.shape
    nblk = s // blk
    def pad_rows(w, r0):
        return jnp.zeros((LANES, w.shape[1]), F32).at[r0:r0 + GLA_LOWRANK].set(w)

    def hi_lo(w):
        hi = w.astype(BF16)
        return hi, (w - hi.astype(F32)).astype(BF16)

    wfh, wfl = hi_lo(pad_rows(wa_f, 0))
    wbh, wbl = hi_lo(pad_rows(wa_b, GLA_LOWRANK))
    sub = min(blk, GLA_SUB)
    r = np.arange(sub)
    same = (r[:, None] // GLA_CHUNK) == (r[None, :] // GLA_CHUNK)
    tri_f = jnp.asarray(same & (r[None, :] <= r[:, None]), BF16)
    tri_b = jnp.asarray(same & (r[None, :] >= r[:, None]), BF16)

    fw = lambda b, h, i: (b, i, h)
    bw = lambda b, h, i: (b, nblk - 1 - i, h)
    fwt = lambda b, h, i: (b, h, i)
    bwt = lambda b, h, i: (b, h, nblk - 1 - i)

    def seq_specs(tok, tok_t):
        return [pl.BlockSpec((1, blk, GLA_DK), tok), pl.BlockSpec((1, blk, GLA_DK), tok),
                pl.BlockSpec((1, blk, GLA_DV), tok), pl.BlockSpec((1, GLA_DV, blk), tok_t),
                pl.BlockSpec((1, blk, LANES), lambda b, h, i, t=tok: (t(b, h, i)[0], t(b, h, i)[1], 0))]

    wspec = pl.BlockSpec((LANES, GLA_DK), lambda b, h, i: (0, h))
    bspec = pl.BlockSpec((1, GLA_DK), lambda b, h, i: (0, h))
    out_f = pl.BlockSpec((1, blk, GLA_DV), fw)
    out_b = pl.BlockSpec((1, blk, GLA_DV), bw)
    o_shape = jax.ShapeDtypeStruct((bsz, s, GLA_HEADS * GLA_DV), BF16)
    return pl.pallas_call(
        _gla_kernel,
        grid=(bsz, GLA_HEADS, nblk),
        in_specs=seq_specs(fw, fwt) + seq_specs(bw, bwt) + [wspec, wspec, bspec, wspec, wspec, bspec,
                                                             _resident((sub, sub)), _resident((sub, sub))],
        out_specs=[out_f, out_b],
        out_shape=[o_shape, o_shape],
        scratch_shapes=[pltpu.VMEM((GLA_DV, GLA_DK), F32), pltpu.VMEM((GLA_DV, GLA_DK), F32)],
        compiler_params=_cparams("arbitrary", "arbitrary", "arbitrary"),
        name="gla",
    )(gq, gk, gv, gvt, lr, gq, gk, gv, gvt, lr,
      wfh, wfl, ba_f.reshape(1, -1), wbh, wbl, ba_b.reshape(1, -1), tri_f, tri_b)


ATTN_SAFE_SHIFT = 40.0
ATTN_NORM_CHUNK = 1024


def _ring(nk, nslot, stages):
    assert nk % nslot == 0
    offs = [off for _, off in stages]

    def static_iter(i):
        for fn, off in stages:
            if 0 <= i + off < nk:
                fn(i + off, (i + off) % nslot)

    def full(g):
        return all(0 <= g * nslot + u + off < nk for u in range(nslot) for off in offs)

    ngroup = nk // nslot
    interior = [g for g in range(ngroup) if full(g)]
    g_lo, g_hi = (interior[0], interior[-1] + 1) if interior else (ngroup, ngroup)
    for i in range(-max(offs), g_lo * nslot):
        static_iter(i)
    if g_hi - g_lo == 1:
        for i in range(g_lo * nslot, g_hi * nslot):
            static_iter(i)
    elif g_hi > g_lo:
        def group(g, carry):
            for u in range(nslot):
                for fn, off in stages:
                    fn(g * nslot + u + off, (u + off) % nslot)
            return carry
        lax.fori_loop(g_lo, g_hi, group, 0)
    for i in range(g_hi * nslot, nk - min(offs)):
        static_iter(i)


def _attn_kernel(lq1_ref, lk1_ref, lq2_ref, lk2_ref, qt_ref, k1_ref, k2_ref, vt_ref, o_ref,
                 s_ref, p_ref, a_ref, mt_ref, m_ref, l_ref, acc_ref, kmax_ref, *, tk, lambda_init):
    k_refs = (k1_ref, k2_ref)
    nkeys = k1_ref.shape[1]
    nk = nkeys // tk
    nslot = p_ref.shape[0]
    dist = nslot // 2
    nslot_slow = s_ref.shape[0]
    dist_slow = nslot_slow // 2
    tq = qt_ref.shape[2]
    nchunk = max(nkeys // ATTN_NORM_CHUNK, 1)
    kc = nkeys // nchunk

    @pl.when(pl.program_id(2) == 0)
    def _():
        lane = lax.broadcasted_iota(jnp.int32, (kc, LANES), 1)
        for mi in range(2):
            def chunk(i, mx):
                kk = k_refs[mi][0, pl.ds(pl.multiple_of(i * kc, kc), kc), :].astype(F32)
                kk = jnp.where(lane < DIFF_HD, kk, 0.0)
                n2 = jnp.sum(kk * kk, axis=1, keepdims=True)
                return jnp.maximum(mx, jnp.max(n2, axis=0, keepdims=True))
            kmax_ref[mi] = jnp.sqrt(lax.fori_loop(0, nchunk, chunk, jnp.zeros((1, 1), F32)))

    qt = qt_ref[0]
    row = lax.broadcasted_iota(jnp.int32, (16, tq), 0)
    q_shift, q_plain, bound = [], [], []
    for mi in range(2):
        qm = qt[mi * DIFF_HD:(mi + 1) * DIFF_HD]
        qf = qm.astype(F32)
        c = jnp.sqrt(jnp.sum(qf * qf, axis=0, keepdims=True)) * kmax_ref[mi]
        c = c.astype(BF16).astype(F32)
        shift_rows = jnp.where(row == 0, -c, 0.0).astype(BF16)
        q_shift.append(jnp.concatenate([qm, shift_rows, jnp.zeros((DIFF_HD - 16, tq), BF16)], axis=0))
        q_plain.append(jnp.concatenate([qm, jnp.zeros((DIFF_HD, tq), BF16)], axis=0))
        bound.append(jnp.max(c))
    safe = jnp.maximum(bound[0], bound[1]) <= ATTN_SAFE_SHIFT
    acc_ref[...] = jnp.zeros_like(acc_ref)
    l_ref[...] = jnp.zeros_like(l_ref)

    def key_tile(mi, t):
        return k_refs[mi][0, pl.ds(pl.multiple_of(t * tk, tk), tk), :]

    def value_tile(t):
        return vt_ref[0, :, pl.ds(pl.multiple_of(t * tk, tk), tk)]

    @pl.when(safe)
    def _():
        def probs(t, slot):
            for mi in range(2):
                p = jnp.exp2(_dot(key_tile(mi, t), q_shift[mi]))
                l_ref[mi] += jnp.sum(p, axis=0, keepdims=True)
                p_ref[slot, mi] = p.astype(BF16)

        def values(t, slot):
            vtj = value_tile(t)
            for mi in range(2):
                acc_ref[mi] += _dot(vtj, p_ref[slot, mi])

        _ring(nk, nslot, [(values, 0), (probs, dist)])

    @pl.when(jnp.logical_not(safe))
    def _():
        m_ref[...] = jnp.full(m_ref.shape, -jnp.inf, F32)

        def scores(t, slot):
            for mi in range(2):
                s = _dot(key_tile(mi, t), q_plain[mi])
                s_ref[slot, mi] = s
                mt_ref[slot, mi] = jnp.max(s, axis=0, keepdims=True)

        def softmax(t, slot):
            for mi in range(2):
                m_old = m_ref[mi]
                m_new = jnp.maximum(m_old, mt_ref[slot, mi])
                a_ref[slot, mi] = jnp.exp2(m_old - m_new)
                p = jnp.exp2(s_ref[slot, mi] - m_new)
                l_ref[mi] = a_ref[slot, mi] * l_ref[mi] + jnp.sum(p, axis=0, keepdims=True)
                p_ref[slot, mi] = p.astype(BF16)
                m_ref[mi] = m_new

        def values(t, slot):
            vtj = value_tile(t)
            for mi in range(2):
                acc_ref[mi] = acc_ref[mi] * a_ref[slot, mi] + _dot(vtj, p_ref[slot, mi])

        _ring(nk, nslot_slow, [(softmax, 0), (values, -dist_slow), (scores, dist_slow)])

    lam = (jnp.exp(jnp.sum(lq1_ref[...] * lk1_ref[...], axis=-1, keepdims=True))
           - jnp.exp(jnp.sum(lq2_ref[...] * lk2_ref[...], axis=-1, keepdims=True)) + lambda_init)
    ot = acc_ref[0] * (1.0 / l_ref[0]) - lam * (acc_ref[1] * (1.0 / l_ref[1]))
    o_ref[0] = ot.T.astype(BF16)


def _attn(dqt, dk1, dk2, dvt, lq1, lk1, lq2, lk2, lambda_init, tq, tk):
    bsz, s, _ = dk1.shape
    lspec = _resident((1, DIFF_HD))
    kspec = pl.BlockSpec((1, s, LANES), lambda b, h, i: (b, 0, h))
    return pl.pallas_call(
        functools.partial(_attn_kernel, tk=tk, lambda_init=lambda_init),
        grid=(bsz, DIFF_HEADS, s // tq),
        in_specs=[lspec, lspec, lspec, lspec,
                  pl.BlockSpec((1, 2 * DIFF_HD, tq), lambda b, h, i: (b, h, i)),
                  kspec, kspec,
                  pl.BlockSpec((1, DIFF_VD, s), lambda b, h, i: (b, h, 0))],
        out_specs=pl.BlockSpec((1, tq, DIFF_VD), lambda b, h, i: (b, i, h)),
        out_shape=jax.ShapeDtypeStruct((bsz, s, DIFF_HEADS * DIFF_VD), BF16),
        scratch_shapes=[pltpu.VMEM((2, 2, tk, tq), F32), pltpu.VMEM((4, 2, tk, tq), BF16),
                        pltpu.VMEM((4, 2, 1, tq), F32), pltpu.VMEM((4, 2, 1, tq), F32),
                        pltpu.VMEM((2, 1, tq), F32), pltpu.VMEM((2, 1, tq), F32),
                        pltpu.VMEM((2, DIFF_VD, tq), F32),
                        pltpu.VMEM((2, 1, 1), F32)],
        compiler_params=_cparams("arbitrary", "arbitrary", "arbitrary"),
        name="attn",
    )(lq1.reshape(1, -1), lk1.reshape(1, -1), lq2.reshape(1, -1), lk2.reshape(1, -1), dqt, dk1, dk2, dvt)


def _merge_kernel(of_ref, ob_ref, gr_ref, od_ref, ga_ref, gb_ref, x_ref, gt_ref, gg_ref, dg_ref,
                  wg_ref, wd_ref, wo_ref, o_ref, *, lambda_init):
    og = of_ref[0].astype(F32) + ob_ref[0].astype(F32)
    gg = gg_ref[...]
    og = jnp.concatenate(
        [_rms(og[:, h * GLA_DV:(h + 1) * GLA_DV], gg[:, h * GLA_DV:(h + 1) * GLA_DV]) for h in range(GLA_HEADS)],
        axis=-1)
    gr = gr_ref[0].astype(F32)
    y_gla = _dot((og * (gr * jax.nn.sigmoid(gr))).astype(BF16), wg_ref[...])
    od = od_ref[0].astype(F32)
    dg = dg_ref[...]
    od = jnp.concatenate(
        [_rms(od[:, h * DIFF_VD:(h + 1) * DIFF_VD], dg[:, h * DIFF_VD:(h + 1) * DIFF_VD]) for h in range(DIFF_HEADS)],
        axis=-1) * (1.0 - lambda_init)
    y_diff = _dot(od.astype(BF16), wd_ref[...])
    merged = (jax.nn.sigmoid(ga_ref[0].astype(F32)) * y_gla + jax.nn.sigmoid(gb_ref[0].astype(F32)) * y_diff)
    o_ref[0] = x_ref[0] + gt_ref[0] * _dot(merged.astype(BF16), wo_ref[...])


def _merge(o_f, o_b, gr, o_d, ga, gb, x, gt1, gla_g, diff_g, wg, wd, wo, lambda_init, tm):
    bsz, s, d = x.shape
    tile = pl.BlockSpec((1, tm, d), lambda b, i: (b, i, 0))
    vec = pl.BlockSpec((1, 1, d), lambda b, i: (b, 0, 0))
    return pl.pallas_call(
        functools.partial(_merge_kernel, lambda_init=lambda_init),
        grid=(bsz, s // tm),
        in_specs=[tile] * 7 + [vec, _resident((1, d)), _resident((1, d)),
                               _resident((d, d)), _resident((d, d)), _resident((d, d))],
        out_specs=tile,
        out_shape=jax.ShapeDtypeStruct((bsz, s, d), F32),
        compiler_params=_cparams("arbitrary", "arbitrary"),
        name="merge",
    )(o_f, o_b, gr, o_d, ga, gb, x, gt1, gla_g.reshape(1, d), diff_g.reshape(1, d), wg, wd, wo)


_CAND_ROW_COUNT = (16, 8, 5, 4, 3, 2, 2, 2)
_SUB = 8
_SENTINEL = 2.0 ** 100


def _top16(a):
    work = a
    vals = []
    for r in range(PEER_TOPK):
        m = jnp.max(work, axis=0, keepdims=True)
        vals.append(m)
        work = jnp.where(work == m, -_SENTINEL * (1.0 + r / 32.0), work)
    rank = jnp.where(work <= -_SENTINEL, (work * (-1.0 / _SENTINEL) - 1.0) * 32.0, float(PEER_TOPK))
    return jnp.concatenate(vals, axis=0), rank


def _route_head(a, b):
    t = a.shape[1]
    top_a, rank_a = _top16(a)
    top_b, rank_b = _top16(b)
    rowi = lax.broadcasted_iota(jnp.int32, (_SUB, t), 0)
    pieces = [top_b + top_a[0:1]]
    for r in range(1, _SUB):
        piece = top_b[0:_SUB] + top_a[r:r + 1]
        n = _CAND_ROW_COUNT[r]
        pieces.append(piece if n == _SUB else jnp.where(rowi < n, piece, -jnp.inf))
    pieces.append(top_a[_SUB:PEER_TOPK] + top_b[0:1])
    cand = jnp.concatenate(pieces, axis=0)
    work = cand
    thr = None
    for _ in range(PEER_TOPK):
        thr = jnp.max(work, axis=0, keepdims=True)
        work = jnp.where(work == thr, -jnp.inf, work)
    sel = cand >= thr
    smax = top_a[0:1] + top_b[0:1]
    z = jnp.sum(jnp.where(sel, jnp.exp(cand - smax), 0.0), axis=0, keepdims=True)
    self32 = sel.astype(F32)
    counts = [jnp.sum(self32[0:PEER_TOPK], axis=0, keepdims=True)]
    for r in range(1, _SUB):
        r0 = PEER_TOPK + (r - 1) * _SUB
        counts.append(jnp.sum(self32[r0:r0 + _SUB], axis=0, keepdims=True))
    tail0 = PEER_TOPK + (_SUB - 1) * _SUB
    for r in range(_SUB, PEER_TOPK):
        counts.append(self32[tail0 + r - _SUB:tail0 + r - _SUB + 1])
    cnt = jnp.zeros(a.shape, F32)
    for r in range(PEER_TOPK):
        cnt = jnp.where(rank_a == float(r), counts[r], cnt)
    wa = jnp.exp(a - top_a[0:1]) * (0.5 / z)
    wb = jnp.exp(b - top_b[0:1])
    return rank_b, wb, cnt, wa


def _route_kernel(x_ref, sh_ref, sc_ref, g_ref, wqt_ref, sk_ref, h2t_ref, rb_ref, wb_ref, cnt_ref, wa_ref):
    h2 = _rms(x_ref[0], g_ref[...]) * (1.0 + sc_ref[0]) + sh_ref[0]
    hb = h2.astype(BF16)
    h2t_ref[0] = h2.T.astype(BF16)
    qt = _dot_nt(wqt_ref[...], hb).astype(BF16)
    for h in range(PEER_HEADS):
        g0 = 2 * h * PEER_NKEYS
        a = _dot(sk_ref[2 * h], qt[g0:g0 + PEER_NKEYS])
        b = _dot(sk_ref[2 * h + 1], qt[g0 + PEER_NKEYS:g0 + 2 * PEER_NKEYS])
        rank_b, wb, cnt, wa = _route_head(a, b)
        rb_ref[0, h] = rank_b.astype(BF16)
        wb_ref[0, h] = wb.astype(BF16)
        cnt_ref[0, h] = cnt
        wa_ref[0, h] = wa


def _route(x1, sh2, sc2, g, wqt, sk, tm):
    bsz, s, d = x1.shape
    vec = pl.BlockSpec((1, 1, d), lambda b, i: (b, 0, 0))
    gate = pl.BlockSpec((1, PEER_HEADS, PEER_NKEYS, tm), lambda b, i: (b, 0, 0, i))
    gshape = lambda dt: jax.ShapeDtypeStruct((bsz, PEER_HEADS, PEER_NKEYS, s), dt)
    return pl.pallas_call(
        _route_kernel,
        grid=(bsz, s // tm),
        in_specs=[pl.BlockSpec((1, tm, d), lambda b, i: (b, i, 0)), vec, vec, _resident((1, d)),
                  _resident(wqt.shape), _resident(sk.shape)],
        out_specs=[pl.BlockSpec((1, d, tm), lambda b, i: (b, 0, i)), gate, gate, gate, gate],
        out_shape=[jax.ShapeDtypeStruct((bsz, d, s), BF16), gshape(BF16), gshape(BF16), gshape(F32), gshape(F32)],
        compiler_params=_cparams("arbitrary", "arbitrary"),
        name="route",
    )(x1, sh2, sc2, g, wqt, sk)


def _dense_kernel(h2t_ref, u_ref, vt_ref, rb_ref, wb_ref, cnt_ref, wa_ref, x_ref, gt_ref, fg_ref, fsh_ref, fsc_ref,
                  o_ref, acc_ref, *, eblk):
    e = pl.program_id(2)

    @pl.when(e == 0)
    def _():
        acc_ref[...] = jnp.zeros_like(acc_ref)

    at = _dot(u_ref[...], h2t_ref[0])
    act = at * (1.0 + lax.erf(at * math.sqrt(0.5)))
    gates = []
    for ii in range(eblk // PEER_NKEYS):
        w = None
        for h in range(PEER_HEADS):
            cnt_i = cnt_ref[0, h, 0, ii:ii + 1, :].astype(BF16)
            wa_i = wa_ref[0, h, 0, ii:ii + 1, :].astype(BF16)
            wh = jnp.where(rb_ref[0, h] < cnt_i, wb_ref[0, h], jnp.zeros((), BF16)) * wa_i
            w = wh if w is None else w + wh
        gates.append(w)
    gate = jnp.concatenate(gates, axis=0)
    acc_ref[...] += _dot(vt_ref[0], act.astype(BF16) * gate)

    @pl.when(e == pl.num_programs(2) - 1)
    def _():
        y = x_ref[0] + gt_ref[0] * acc_ref[...].T
        o_ref[0] = _rms(y, fg_ref[...]) * (1.0 + fsc_ref[0]) + fsh_ref[0]


def _dense(h2t, u, vt, rb, wb, cnt, wa, x1, gt2, fg, fsh, fsc, tm, eblk):
    bsz, s, d = x1.shape
    nblk = u.shape[0] // eblk
    gate = pl.BlockSpec((1, PEER_HEADS, PEER_NKEYS, tm), lambda b, i, e: (b, 0, 0, i))
    nrow = eblk // PEER_NKEYS
    rows = pl.BlockSpec((1, PEER_HEADS, 1, nrow, tm), lambda b, i, e: (b, 0, e, 0, i))
    by_block = lambda a: a.reshape(bsz, PEER_HEADS, nblk, nrow, s)
    tile = pl.BlockSpec((1, tm, d), lambda b, i, e: (b, i, 0))
    vec = pl.BlockSpec((1, 1, d), lambda b, i, e: (b, 0, 0))
    return pl.pallas_call(
        functools.partial(_dense_kernel, eblk=eblk),
        grid=(bsz, s // tm, nblk),
        in_specs=[pl.BlockSpec((1, d, tm), lambda b, i, e: (b, 0, i)),
                  pl.BlockSpec((eblk, d), lambda b, i, e: (e, 0)),
                  pl.BlockSpec((1, d, eblk), lambda b, i, e: (e, 0, 0)),
                  gate, gate, rows, rows, tile, vec, _resident((1, d)), vec, vec],
        out_specs=tile,
        out_shape=jax.ShapeDtypeStruct((bsz, s, d), F32),
        scratch_shapes=[pltpu.VMEM((d, tm), F32)],
        compiler_params=_cparams("arbitrary", "arbitrary", "arbitrary"),
        name="dense",
    )(h2t, u, vt, rb, wb, by_block(cnt), by_block(wa), x1, gt2, fg.reshape(1, d), fsh, fsc)


def _tile(s, want):
    return want if s % want == 0 else s


def kernel(x, c, positions, w_ada, b_ada, norm1_g, w_in, gla_wa_fw, gla_ba_fw, gla_wa_bw, gla_ba_bw, gla_norm_g,
           diff_lq1, diff_lk1, diff_lq2, diff_lk2, diff_norm_g, w_gla_proj, w_diff_proj, w_out, norm2_g, peer_wq,
           peer_subkeys, peer_u, peer_v, w_final_ada, b_final_ada, normf_g):
    bsz, s, d = x.shape
    depth = w_ada.shape[0]
    tm = _tile(s, 512)

    half = DIFF_HD // 2
    inv = (ROPE_THETA ** (-np.arange(0, DIFF_HD, 2, dtype=np.float32) / DIFF_HD)).astype(np.float32)
    inv_row = jnp.asarray(np.tile(inv, LANES // half).reshape(1, LANES))
    inv_col = jnp.asarray(inv.reshape(half, 1))

    fmod = _ada(c, w_final_ada, b_final_ada)
    f_shift = fmod[:, :d].reshape(bsz, 1, d)
    f_scale = fmod[:, d:].reshape(bsz, 1, d)

    qk, gv_w, lr_w = 2 * GLA_HEADS * GLA_DK, GLA_HEADS * GLA_DV, 2 * GLA_LOWRANK
    o_gq, o_gk, o_gv, o_gr = 0, qk // 2, qk, qk + gv_w
    o_lr = o_gr + gv_w
    o_dq = o_lr + lr_w
    o_dk, o_dv, o_ga, o_gb = o_dq + d, o_dq + 2 * d, o_dq + 3 * d, o_dq + 4 * d

    for l in range(depth):
        lambda_init = 0.8 - 0.6 * math.exp(-0.3 * l)
        mod = _ada(c, w_ada[l], b_ada[l]).reshape(bsz, 1, N_ADA * d)
        sh1, sc1, gt1, sh2, sc2, gt2 = (mod[:, :, i * d:(i + 1) * d] for i in range(N_ADA))

        w = w_in[l]
        col = lambda c0, n: w[:, c0:c0 + n]
        wn = jnp.concatenate([col(o_gq, qk // 2), col(o_gk, qk // 2), col(o_gv, gv_w), col(o_gr, gv_w),
                              col(o_dk, d), col(o_ga, d), col(o_gb, d)], axis=1).astype(BF16)
        wlr = jnp.zeros((d, LANES), F32).at[:, :lr_w].set(col(o_lr, lr_w)).astype(BF16)
        wt = jnp.concatenate([col(o_dq, d), col(o_dv, d), col(o_gv, gv_w)], axis=1).T.astype(BF16)
        (gq, gk, gv, gr, lr, dk1, dk2, ga, gb, dqt, dvt, gvt) = _inproj(
            x, sh1, sc1, norm1_g[l].reshape(1, d), positions, wn, wlr, wt, inv_row, inv_col, tm)

        o_f, o_b = _gla(gq, gk, gv, gvt, lr, gla_wa_fw[l], gla_ba_fw[l], gla_wa_bw[l], gla_ba_bw[l], _tile(s, 512))
        o_d = _attn(dqt, dk1, dk2, dvt, diff_lq1[l], diff_lk1[l], diff_lq2[l], diff_lk2[l], lambda_init,
                    _tile(s, 2048), _tile(s, 256))
        x1 = _merge(o_f, o_b, gr, o_d, ga, gb, x, gt1, gla_norm_g[l], diff_norm_g[l],
                    w_gla_proj[l].astype(BF16), w_diff_proj[l].astype(BF16), w_out[l].astype(BF16), lambda_init, tm)

        sk = peer_subkeys[l].reshape(2 * PEER_HEADS, PEER_NKEYS, -1).astype(BF16)
        h2t, rb, wb, cnt, wa = _route(x1, sh2, sc2, norm2_g[l].reshape(1, d), peer_wq[l].T.astype(BF16), sk,
                                      _tile(s, 256))
        last = l == depth - 1
        assert last, "multi-layer stacks need the final norm split out of the expert kernel"
        eblk = 8 * PEER_NKEYS
        vt = peer_v[l].astype(BF16).reshape(-1, eblk, d).transpose(0, 2, 1)
        x = _dense(h2t, peer_u[l].astype(BF16), vt, rb, wb, cnt, wa, x1, gt2,
                   normf_g, f_shift, f_scale, tm, eblk)
    return x
```

```python
import functools
import math

import numpy as np
import jax
import jax.numpy as jnp
from jax import lax
from jax.experimental import pallas as pl
from jax.experimental.pallas import tpu as pltpu

F32 = jnp.float32
BF16 = jnp.bfloat16

RMS_EPS = 1e-6
ROPE_THETA = 10000.0
GLA_HEADS = 4
GLA_DK = 128
GLA_DV = 256
GLA_LOWRANK = 16
GLA_GATE_NORM = 16.0
GLA_CHUNK = 64
GLA_SUB = 256
DIFF_HEADS = 8
DIFF_HD = 64
DIFF_VD = 128
PEER_HEADS = 8
PEER_NKEYS = 128
PEER_TOPK = 16
N_ADA = 6

LANES = 128
VMEM_LIMIT = 48 * 1024 * 1024

NT_DIMS = (((1,), (1,)), ((), ()))


def _cparams(*sem):
    return pltpu.CompilerParams(dimension_semantics=sem, vmem_limit_bytes=VMEM_LIMIT)


def _dot(a, b):
    return jnp.dot(a, b, preferred_element_type=F32)


def _dot_nt(a, b):
    return lax.dot_general(a, b, NT_DIMS, preferred_element_type=F32)


def _rms(x, g):
    ms = jnp.mean(x * x, axis=-1, keepdims=True)
    return x * lax.rsqrt(ms + RMS_EPS) * g


def _resident(shape):
    nd = len(shape)
    return pl.BlockSpec(shape, lambda *_: (0,) * nd)


def _ada_kernel(c_ref, w_ref, b_ref, o_ref):
    c = c_ref[...]
    ca = c * jax.nn.sigmoid(c)
    o_ref[...] = jnp.dot(ca, w_ref[...], preferred_element_type=F32,
                         precision=lax.Precision.HIGHEST) + b_ref[...]


def _ada(c, w, b):
    bsz, d = c.shape
    n = w.shape[1]
    tn = 2048
    return pl.pallas_call(
        _ada_kernel,
        grid=(n // tn,),
        in_specs=[pl.BlockSpec((bsz, d), lambda j: (0, 0)),
                  pl.BlockSpec((d, tn), lambda j: (0, j)),
                  pl.BlockSpec((1, tn), lambda j: (0, j))],
        out_specs=pl.BlockSpec((bsz, tn), lambda j: (0, j)),
        out_shape=jax.ShapeDtypeStruct((bsz, n), F32),
        compiler_params=_cparams("arbitrary"),
        name="ada",
    )(c, w, b.reshape(1, n))


_C_GQ, _C_GK, _C_GV, _C_GR, _C_DK, _C_GA, _C_GB, _C_END = 0, 512, 1024, 2048, 3072, 4096, 5120, 6144
_R_DQ, _R_DV, _R_GV, _R_END = 0, 1024, 2048, 3072


def _inproj_kernel(x_ref, sh_ref, sc_ref, g_ref, pos_ref, post_ref, invr_ref, invc_ref,
                   wn_ref, wlr_ref, wt_ref,
                   gq_ref, gk_ref, gv_ref, gr_ref, lr_ref, dk1_ref, dk2_ref, ga_ref, gb_ref,
                   dqt_ref, dvt_ref, gvt_ref):
    tm = x_ref.shape[1]
    h = _rms(x_ref[0], g_ref[...]) * (1.0 + sc_ref[0]) + sh_ref[0]
    hb = h.astype(BF16)

    def proj(c0, c1):
        return _dot(hb, wn_ref[:, c0:c1])

    gq_ref[0] = (proj(_C_GQ, _C_GK) * (GLA_DK ** -0.5)).astype(BF16)
    gk_ref[0] = proj(_C_GK, _C_GV).astype(BF16)
    gv_ref[0] = proj(_C_GV, _C_GR).astype(BF16)
    gr_ref[0] = proj(_C_GR, _C_DK).astype(BF16)
    ga_ref[0] = proj(_C_GA, _C_GB).astype(BF16)
    gb_ref[0] = proj(_C_GB, _C_END).astype(BF16)
    lr_ref[0] = _dot(hb, wlr_ref[...])

    ang = pos_ref[0] * invr_ref[...]
    cs = jnp.cos(ang)
    sn = jnp.sin(ang)
    lane = lax.broadcasted_iota(jnp.int32, (tm, LANES), 1)
    first = (lane % DIFF_HD) < (DIFF_HD // 2)
    sn = jnp.where(first, -sn, sn)
    y = proj(_C_DK, _C_GA)
    pad = jnp.where(lane == DIFF_HD, 1.0, 0.0)
    for hd in range(DIFF_HEADS):
        yb = y[:, hd * LANES:(hd + 1) * LANES]
        rot = jnp.where(first, pltpu.roll(yb, LANES - DIFF_HD // 2, 1), pltpu.roll(yb, DIFF_HD // 2, 1))
        kr = yb * cs + rot * sn
        dk1_ref[0, :, hd * LANES:(hd + 1) * LANES] = jnp.where(lane < DIFF_HD, kr, pad).astype(BF16)
        dk2_ref[0, :, hd * LANES:(hd + 1) * LANES] = jnp.where(lane < DIFF_HD, pltpu.roll(kr, DIFF_HD, 1),
                                                                pad).astype(BF16)

    angt = invc_ref[...] * post_ref[0]
    ct = jnp.cos(angt)
    st = jnp.sin(angt)
    half = DIFF_HD // 2
    yt = _dot_nt(wt_ref[_R_DQ:_R_DV, :], hb)
    qscale = DIFF_HD ** -0.5 * math.log2(math.e)
    for blk in range(2 * DIFF_HEADS):
        r0 = blk * DIFF_HD
        x1 = yt[r0:r0 + half]
        x2 = yt[r0 + half:r0 + DIFF_HD]
        dqt_ref[0, r0:r0 + half, :] = ((x1 * ct - x2 * st) * qscale).astype(BF16)
        dqt_ref[0, r0 + half:r0 + DIFF_HD, :] = ((x2 * ct + x1 * st) * qscale).astype(BF16)
    dvt_ref[0] = _dot_nt(wt_ref[_R_DV:_R_GV, :], hb).astype(BF16)
    gvt_ref[0] = _dot_nt(wt_ref[_R_GV:_R_END, :], hb).astype(BF16)


def _inproj(x, sh, sc, g, pos, wn, wlr, wt, inv_row, inv_col, tm):
    bsz, s, d = x.shape
    posf = pos.astype(F32)
    nat = lambda n: pl.BlockSpec((1, tm, n), lambda b, i: (b, i, 0))
    tr = lambda n: pl.BlockSpec((1, n, tm), lambda b, i: (b, 0, i))
    vec = pl.BlockSpec((1, 1, d), lambda b, i: (b, 0, 0))
    shp = lambda *dims: jax.ShapeDtypeStruct(dims, BF16)
    return pl.pallas_call(
        _inproj_kernel,
        grid=(bsz, s // tm),
        in_specs=[nat(d), vec, vec, _resident((1, d)),
                  pl.BlockSpec((1, tm, 1), lambda b, i: (b, i, 0)),
                  pl.BlockSpec((1, 1, tm), lambda b, i: (b, 0, i)),
                  _resident(inv_row.shape), _resident(inv_col.shape),
                  _resident(wn.shape), _resident(wlr.shape), _resident(wt.shape)],
        out_specs=[nat(512), nat(512), nat(1024), nat(1024), nat(LANES), nat(1024), nat(1024), nat(1024), nat(1024),
                   tr(1024), tr(1024), tr(1024)],
        out_shape=[shp(bsz, s, 512), shp(bsz, s, 512), shp(bsz, s, 1024), shp(bsz, s, 1024),
                   jax.ShapeDtypeStruct((bsz, s, LANES), F32),
                   shp(bsz, s, 1024), shp(bsz, s, 1024), shp(bsz, s, 1024), shp(bsz, s, 1024),
                   shp(bsz, 1024, s), shp(bsz, 1024, s), shp(bsz, 1024, s)],
        compiler_params=_cparams("arbitrary", "arbitrary"),
        name="inproj",
    )(x, sh, sc, g, posf.reshape(bsz, s, 1), posf.reshape(bsz, 1, s), inv_row, inv_col, wn, wlr, wt)


def _gla_direction(q_ref, k_ref, v_ref, vt_ref, lr_ref, wh_ref, wl_ref, ba_ref, tri_ref, o_ref, s_ref, reverse):
    blk = q_ref.shape[1]
    lr = lr_ref[0]
    lr_hi = lr.astype(BF16)
    lr_lo = (lr - lr_hi.astype(F32)).astype(BF16)
    z = _dot(lr_hi, wh_ref[...]) + _dot(lr_lo, wh_ref[...]) + _dot(lr_hi, wl_ref[...]) + ba_ref[...]
    logg = -(jnp.maximum(-z, 0.0) + jnp.log1p(jnp.exp(-jnp.abs(z)))) * (1.0 / GLA_GATE_NORM)
    lg_hi = logg.astype(BF16)
    lg_lo = (logg - lg_hi.astype(F32)).astype(BF16)
    yield
    tri = tri_ref[...]
    sub = tri.shape[0]
    q = q_ref[0].astype(F32)
    k = k_ref[0].astype(F32)
    v = v_ref[0]
    b, qe, o_intra = [], [], []
    for r0 in range(0, blk, sub):
        rows = slice(r0, r0 + sub)
        b_s = _dot(tri, lg_hi[rows]) + _dot(tri, lg_lo[rows])
        qe_s = (q[rows] * jnp.exp(b_s)).astype(BF16)
        ke_s = (k[rows] * jnp.exp(-b_s)).astype(BF16)
        att = jnp.where(tri > 0, _dot_nt(qe_s, ke_s), 0.0).astype(BF16)
        b.append(b_s)
        qe.append(qe_s)
        o_intra.append(_dot(att, v[rows]))
        yield
    b = jnp.concatenate(b, axis=0)
    qe = jnp.concatenate(qe, axis=0)
    o_intra = jnp.concatenate(o_intra, axis=0)

    state = s_ref[...]
    nchunk = blk // GLA_CHUNK
    order = range(nchunk - 1, -1, -1) if reverse else range(nchunk)
    for n in order:
        r0 = n * GLA_CHUNK
        edge = r0 if reverse else r0 + GLA_CHUNK - 1
        b_c = b[r0:r0 + GLA_CHUNK]
        b_e = b[edge:edge + 1]
        kend = (k[r0:r0 + GLA_CHUNK] * jnp.exp(b_e - b_c)).astype(BF16)
        o_inter = _dot_nt(qe[r0:r0 + GLA_CHUNK], state.astype(BF16))
        o_ref[0, r0:r0 + GLA_CHUNK, :] = (o_intra[r0:r0 + GLA_CHUNK] + o_inter).astype(BF16)
        state = state * jnp.exp(b_e) + _dot(vt_ref[0, :, r0:r0 + GLA_CHUNK], kend)
        yield
    s_ref[...] = state


def _gla_kernel(qf, kf, vf, vtf, lrf, qb, kb, vb, vtb, lrb,
                wfh, wfl, baf, wbh, wbl, bab, trif, trib,
                of_ref, ob_ref, sf_ref, sb_ref):
    @pl.when(pl.program_id(2) == 0)
    def _():
        sf_ref[...] = jnp.zeros_like(sf_ref)
        sb_ref[...] = jnp.zeros_like(sb_ref)

    live = [_gla_direction(qf, kf, vf, vtf, lrf, wfh, wfl, baf, trif, of_ref, sf_ref, reverse=False),
            _gla_direction(qb, kb, vb, vtb, lrb, wbh, wbl, bab, trib, ob_ref, sb_ref, reverse=True)]
    while live:
        live = [d for d in live if next(d, True) is None]


def _gla(gq, gk, gv, gvt, lr, wa_f, ba_f, wa_b, ba_b, blk):
    bsz, s, _ = gq.shape
    nblk = s // blk
    def pad_rows(w, r0):
        return jnp.zeros((LANES, w.shape[1]), F32).at[r0:r0 + GLA_LOWRANK].set(w)

    def hi_lo(w):
        hi = w.astype(BF16)
        return hi, (w - hi.astype(F32)).astype(BF16)

    wfh, wfl = hi_lo(pad_rows(wa_f, 0))
    wbh, wbl = hi_lo(pad_rows(wa_b, GLA_LOWRANK))
    sub = min(blk, GLA_SUB)
    r = np.arange(sub)
    same = (r[:, None] // GLA_CHUNK) == (r[None, :] // GLA_CHUNK)
    tri_f = jnp.asarray(same & (r[None, :] <= r[:, None]), BF16)
    tri_b = jnp.asarray(same & (r[None, :] >= r[:, None]), BF16)

    fw = lambda b, h, i: (b, i, h)
    bw = lambda b, h, i: (b, nblk - 1 - i, h)
    fwt = lambda b, h, i: (b, h, i)
    bwt = lambda b, h, i: (b, h, nblk - 1 - i)

    def seq_specs(tok, tok_t):
        return [pl.BlockSpec((1, blk, GLA_DK), tok), pl.BlockSpec((1, blk, GLA_DK), tok),
                pl.BlockSpec((1, blk, GLA_DV), tok), pl.BlockSpec((1, GLA_DV, blk), tok_t),
                pl.BlockSpec((1, blk, LANES), lambda b, h, i, t=tok: (t(b, h, i)[0], t(b, h, i)[1], 0))]

    wspec = pl.BlockSpec((LANES, GLA_DK), lambda b, h, i: (0, h))
    bspec = pl.BlockSpec((1, GLA_DK), lambda b, h, i: (0, h))
    out_f = pl.BlockSpec((1, blk, GLA_DV), fw)
    out_b = pl.BlockSpec((1, blk, GLA_DV), bw)
    o_shape = jax.ShapeDtypeStruct((bsz, s, GLA_HEADS * GLA_DV), BF16)
    return pl.pallas_call(
        _gla_kernel,
        grid=(bsz, GLA_HEADS, nblk),
        in_specs=seq_specs(fw, fwt) + seq_specs(bw, bwt) + [wspec, wspec, bspec, wspec, wspec, bspec,
                                                             _resident((sub, sub)), _resident((sub, sub))],
        out_specs=[out_f, out_b],
        out_shape=[o_shape, o_shape],
        scratch_shapes=[pltpu.VMEM((GLA_DV, GLA_DK), F32), pltpu.VMEM((GLA_DV, GLA_DK), F32)],
        compiler_params=_cparams("arbitrary", "arbitrary", "arbitrary"),
        name="gla",
    )(gq, gk, gv, gvt, lr, gq, gk, gv, gvt, lr,
      wfh, wfl, ba_f.reshape(1, -1), wbh, wbl, ba_b.reshape(1, -1), tri_f, tri_b)


ATTN_SAFE_SHIFT = 40.0
ATTN_NORM_CHUNK = 1024


def _ring(nk, nslot, stages):
    assert nk % nslot == 0
    offs = [off for _, off in stages]

    def static_iter(i):
        for fn, off in stages:
            if 0 <= i + off < nk:
                fn(i + off, (i + off) % nslot)

    def full(g):
        return all(0 <= g * nslot + u + off < nk for u in range(nslot) for off in offs)

    ngroup = nk // nslot
    interior = [g for g in range(ngroup) if full(g)]
    g_lo, g_hi = (interior[0], interior[-1] + 1) if interior else (ngroup, ngroup)
    for i in range(-max(offs), g_lo * nslot):
        static_iter(i)
    if g_hi - g_lo == 1:
        for i in range(g_lo * nslot, g_hi * nslot):
            static_iter(i)
    elif g_hi > g_lo:
        def group(g, carry):
            for u in range(nslot):
                for fn, off in stages:
                    fn(g * nslot + u + off, (u + off) % nslot)
            return carry
        lax.fori_loop(g_lo, g_hi, group, 0)
    for i in range(g_hi * nslot, nk - min(offs)):
        static_iter(i)


def _attn_kernel(lq1_ref, lk1_ref, lq2_ref, lk2_ref, qt_ref, k1_ref, k2_ref, vt_ref, o_ref,
                 s_ref, p_ref, a_ref, mt_ref, m_ref, l_ref, acc_ref, kmax_ref, *, tk, lambda_init):
    k_refs = (k1_ref, k2_ref)
    nkeys = k1_ref.shape[1]
    nk = nkeys // tk
    nslot = p_ref.shape[0]
    dist = nslot // 2
    nslot_slow = s_ref.shape[0]
    dist_slow = nslot_slow // 2
    tq = qt_ref.shape[2]
    nchunk = max(nkeys // ATTN_NORM_CHUNK, 1)
    kc = nkeys // nchunk

    @pl.when(pl.program_id(2) == 0)
    def _():
        lane = lax.broadcasted_iota(jnp.int32, (kc, LANES), 1)
        for mi in range(2):
            def chunk(i, mx):
                kk = k_refs[mi][0, pl.ds(pl.multiple_of(i * kc, kc), kc), :].astype(F32)
                kk = jnp.where(lane < DIFF_HD, kk, 0.0)
                n2 = jnp.sum(kk * kk, axis=1, keepdims=True)
                return jnp.maximum(mx, jnp.max(n2, axis=0, keepdims=True))
            kmax_ref[mi] = jnp.sqrt(lax.fori_loop(0, nchunk, chunk, jnp.zeros((1, 1), F32)))

    qt = qt_ref[0]
    row = lax.broadcasted_iota(jnp.int32, (16, tq), 0)
    q_shift, q_plain, bound = [], [], []
    for mi in range(2):
        qm = qt[mi * DIFF_HD:(mi + 1) * DIFF_HD]
        qf = qm.astype(F32)
        c = jnp.sqrt(jnp.sum(qf * qf, axis=0, keepdims=True)) * kmax_ref[mi]
        c = c.astype(BF16).astype(F32)
        shift_rows = jnp.where(row == 0, -c, 0.0).astype(BF16)
        q_shift.append(jnp.concatenate([qm, shift_rows, jnp.zeros((DIFF_HD - 16, tq), BF16)], axis=0))
        q_plain.append(jnp.concatenate([qm, jnp.zeros((DIFF_HD, tq), BF16)], axis=0))
        bound.append(jnp.max(c))
    safe = jnp.maximum(bound[0], bound[1]) <= ATTN_SAFE_SHIFT
    acc_ref[...] = jnp.zeros_like(acc_ref)
    l_ref[...] = jnp.zeros_like(l_ref)

    def key_tile(mi, t):
        return k_refs[mi][0, pl.ds(pl.multiple_of(t * tk, tk), tk), :]

    def value_tile(t):
        return vt_ref[0, :, pl.ds(pl.multiple_of(t * tk, tk), tk)]

    @pl.when(safe)
    def _():
        def probs(t, slot):
            for mi in range(2):
                p = jnp.exp2(_dot(key_tile(mi, t), q_shift[mi]))
                l_ref[mi] += jnp.sum(p, axis=0, keepdims=True)
                p_ref[slot, mi] = p.astype(BF16)

        def values(t, slot):
            vtj = value_tile(t)
            for mi in range(2):
                acc_ref[mi] += _dot(vtj, p_ref[slot, mi])

        _ring(nk, nslot, [(values, 0), (probs, dist)])

    @pl.when(jnp.logical_not(safe))
    def _():
        m_ref[...] = jnp.full(m_ref.shape, -jnp.inf, F32)

        def scores(t, slot):
            for mi in range(2):
                s = _dot(key_tile(mi, t), q_plain[mi])
                s_ref[slot, mi] = s
                mt_ref[slot, mi] = jnp.max(s, axis=0, keepdims=True)

        def softmax(t, slot):
            for mi in range(2):
                m_old = m_ref[mi]
                m_new = jnp.maximum(m_old, mt_ref[slot, mi])
                a_ref[slot, mi] = jnp.exp2(m_old - m_new)
                p = jnp.exp2(s_ref[slot, mi] - m_new)
                l_ref[mi] = a_ref[slot, mi] * l_ref[mi] + jnp.sum(p, axis=0, keepdims=True)
                p_ref[slot, mi] = p.astype(BF16)
                m_ref[mi] = m_new

        def values(t, slot):
            vtj = value_tile(t)
            for mi in range(2):
                acc_ref[mi] = acc_ref[mi] * a_ref[slot, mi] + _dot(vtj, p_ref[slot, mi])

        _ring(nk, nslot_slow, [(softmax, 0), (values, -dist_slow), (scores, dist_slow)])

    lam = (jnp.exp(jnp.sum(lq1_ref[...] * lk1_ref[...], axis=-1, keepdims=True))
           - jnp.exp(jnp.sum(lq2_ref[...] * lk2_ref[...], axis=-1, keepdims=True)) + lambda_init)
    ot = acc_ref[0] * (1.0 / l_ref[0]) - lam * (acc_ref[1] * (1.0 / l_ref[1]))
    o_ref[0] = ot.T.astype(BF16)


def _attn(dqt, dk1, dk2, dvt, lq1, lk1, lq2, lk2, lambda_init, tq, tk):
    bsz, s, _ = dk1.shape
    lspec = _resident((1, DIFF_HD))
    kspec = pl.BlockSpec((1, s, LANES), lambda b, h, i: (b, 0, h))
    return pl.pallas_call(
        functools.partial(_attn_kernel, tk=tk, lambda_init=lambda_init),
        grid=(bsz, DIFF_HEADS, s // tq),
        in_specs=[lspec, lspec, lspec, lspec,
                  pl.BlockSpec((1, 2 * DIFF_HD, tq), lambda b, h, i: (b, h, i)),
                  kspec, kspec,
                  pl.BlockSpec((1, DIFF_VD, s), lambda b, h, i: (b, h, 0))],
        out_specs=pl.BlockSpec((1, tq, DIFF_VD), lambda b, h, i: (b, i, h)),
        out_shape=jax.ShapeDtypeStruct((bsz, s, DIFF_HEADS * DIFF_VD), BF16),
        scratch_shapes=[pltpu.VMEM((2, 2, tk, tq), F32), pltpu.VMEM((4, 2, tk, tq), BF16),
                        pltpu.VMEM((4, 2, 1, tq), F32), pltpu.VMEM((4, 2, 1, tq), F32),
                        pltpu.VMEM((2, 1, tq), F32), pltpu.VMEM((2, 1, tq), F32),
                        pltpu.VMEM((2, DIFF_VD, tq), F32),
                        pltpu.VMEM((2, 1, 1), F32)],
        compiler_params=_cparams("arbitrary", "arbitrary", "arbitrary"),
        name="attn",
    )(lq1.reshape(1, -1), lk1.reshape(1, -1), lq2.reshape(1, -1), lk2.reshape(1, -1), dqt, dk1, dk2, dvt)


def _merge_kernel(of_ref, ob_ref, gr_ref, od_ref, ga_ref, gb_ref, x_ref, gt_ref, gg_ref, dg_ref,
                  wg_ref, wd_ref, wo_ref, o_ref, *, lambda_init):
    og = of_ref[0].astype(F32) + ob_ref[0].astype(F32)
    gg = gg_ref[...]
    og = jnp.concatenate(
        [_rms(og[:, h * GLA_DV:(h + 1) * GLA_DV], gg[:, h * GLA_DV:(h + 1) * GLA_DV]) for h in range(GLA_HEADS)],
        axis=-1)
    gr = gr_ref[0].astype(F32)
    y_gla = _dot((og * (gr * jax.nn.sigmoid(gr))).astype(BF16), wg_ref[...])
    od = od_ref[0].astype(F32)
    dg = dg_ref[...]
    od = jnp.concatenate(
        [_rms(od[:, h * DIFF_VD:(h + 1) * DIFF_VD], dg[:, h * DIFF_VD:(h + 1) * DIFF_VD]) for h in range(DIFF_HEADS)],
        axis=-1) * (1.0 - lambda_init)
    y_diff = _dot(od.astype(BF16), wd_ref[...])
    merged = (jax.nn.sigmoid(ga_ref[0].astype(F32)) * y_gla + jax.nn.sigmoid(gb_ref[0].astype(F32)) * y_diff)
    o_ref[0] = x_ref[0] + gt_ref[0] * _dot(merged.astype(BF16), wo_ref[...])


def _merge(o_f, o_b, gr, o_d, ga, gb, x, gt1, gla_g, diff_g, wg, wd, wo, lambda_init, tm):
    bsz, s, d = x.shape
    tile = pl.BlockSpec((1, tm, d), lambda b, i: (b, i, 0))
    vec = pl.BlockSpec((1, 1, d), lambda b, i: (b, 0, 0))
    return pl.pallas_call(
        functools.partial(_merge_kernel, lambda_init=lambda_init),
        grid=(bsz, s // tm),
        in_specs=[tile] * 7 + [vec, _resident((1, d)), _resident((1, d)),
                               _resident((d, d)), _resident((d, d)), _resident((d, d))],
        out_specs=tile,
        out_shape=jax.ShapeDtypeStruct((bsz, s, d), F32),
        compiler_params=_cparams("arbitrary", "arbitrary"),
        name="merge",
    )(o_f, o_b, gr, o_d, ga, gb, x, gt1, gla_g.reshape(1, d), diff_g.reshape(1, d), wg, wd, wo)


_CAND_ROW_COUNT = (16, 8, 5, 4, 3, 2, 2, 2)
_SUB = 8
_SENTINEL = 2.0 ** 100


def _top16(a):
    work = a
    vals = []
    for r in range(PEER_TOPK):
        m = jnp.max(work, axis=0, keepdims=True)
        vals.append(m)
        work = jnp.where(work == m, -_SENTINEL * (1.0 + r / 32.0), work)
    rank = jnp.where(work <= -_SENTINEL, (work * (-1.0 / _SENTINEL) - 1.0) * 32.0, float(PEER_TOPK))
    return jnp.concatenate(vals, axis=0), rank


def _route_head(a, b):
    t = a.shape[1]
    top_a, rank_a = _top16(a)
    top_b, rank_b = _top16(b)
    rowi = lax.broadcasted_iota(jnp.int32, (_SUB, t), 0)
    pieces = [top_b + top_a[0:1]]
    for r in range(1, _SUB):
        piece = top_b[0:_SUB] + top_a[r:r + 1]
        n = _CAND_ROW_COUNT[r]
        pieces.append(piece if n == _SUB else jnp.where(rowi < n, piece, -jnp.inf))
    pieces.append(top_a[_SUB:PEER_TOPK] + top_b[0:1])
    cand = jnp.concatenate(pieces, axis=0)
    work = cand
    thr = None
    for _ in range(PEER_TOPK):
        thr = jnp.max(work, axis=0, keepdims=True)
        work = jnp.where(work == thr, -jnp.inf, work)
    sel = cand >= thr
    smax = top_a[0:1] + top_b[0:1]
    z = jnp.sum(jnp.where(sel, jnp.exp(cand - smax), 0.0), axis=0, keepdims=True)
    self32 = sel.astype(F32)
    counts = [jnp.sum(self32[0:PEER_TOPK], axis=0, keepdims=True)]
    for r in range(1, _SUB):
        r0 = PEER_TOPK + (r - 1) * _SUB
        counts.append(jnp.sum(self32[r0:r0 + _SUB], axis=0, keepdims=True))
    tail0 = PEER_TOPK + (_SUB - 1) * _SUB
    for r in range(_SUB, PEER_TOPK):
        counts.append(self32[tail0 + r - _SUB:tail0 + r - _SUB + 1])
    cnt = jnp.zeros(a.shape, F32)
    for r in range(PEER_TOPK):
        cnt = jnp.where(rank_a == float(r), counts[r], cnt)
    wa = jnp.exp(a - top_a[0:1]) * (0.5 / z)
    wb = jnp.exp(b - top_b[0:1])
    return rank_b, wb, cnt, wa


def _route_kernel(x_ref, sh_ref, sc_ref, g_ref, wqt_ref, sk_ref, h2t_ref, rb_ref, wb_ref, cnt_ref, wa_ref):
    h2 = _rms(x_ref[0], g_ref[...]) * (1.0 + sc_ref[0]) + sh_ref[0]
    hb = h2.astype(BF16)
    h2t_ref[0] = h2.T.astype(BF16)
    qt = _dot_nt(wqt_ref[...], hb).astype(BF16)
    for h in range(PEER_HEADS):
        g0 = 2 * h * PEER_NKEYS
        a = _dot(sk_ref[2 * h], qt[g0:g0 + PEER_NKEYS])
        b = _dot(sk_ref[2 * h + 1], qt[g0 + PEER_NKEYS:g0 + 2 * PEER_NKEYS])
        rank_b, wb, cnt, wa = _route_head(a, b)
        rb_ref[0, h] = rank_b.astype(BF16)
        wb_ref[0, h] = wb.astype(BF16)
        cnt_ref[0, h] = cnt
        wa_ref[0, h] = wa


def _route(x1, sh2, sc2, g, wqt, sk, tm):
    bsz, s, d = x1.shape
    vec = pl.BlockSpec((1, 1, d), lambda b, i: (b, 0, 0))
    gate = pl.BlockSpec((1, PEER_HEADS, PEER_NKEYS, tm), lambda b, i: (b, 0, 0, i))
    gshape = lambda dt: jax.ShapeDtypeStruct((bsz, PEER_HEADS, PEER_NKEYS, s), dt)
    return pl.pallas_call(
        _route_kernel,
        grid=(bsz, s // tm),
        in_specs=[pl.BlockSpec((1, tm, d), lambda b, i: (b, i, 0)), vec, vec, _resident((1, d)),
                  _resident(wqt.shape), _resident(sk.shape)],
        out_specs=[pl.BlockSpec((1, d, tm), lambda b, i: (b, 0, i)), gate, gate, gate, gate],
        out_shape=[jax.ShapeDtypeStruct((bsz, d, s), BF16), gshape(BF16), gshape(BF16), gshape(F32), gshape(F32)],
        compiler_params=_cparams("arbitrary", "arbitrary"),
        name="route",
    )(x1, sh2, sc2, g, wqt, sk)


DENSE_ROWS = PEER_NKEYS


def _dense_kernel(h2t_ref, u_ref, vt_ref, rb_ref, wb_ref, cnt_ref, wa_ref, x_ref, gt_ref, fg_ref, fsh_ref, fsc_ref,
                  o_ref, acc_ref, p_ref, *, eblk):
    e = pl.program_id(2)

    @pl.when(e == 0)
    def _():
        acc_ref[...] = jnp.zeros_like(acc_ref)

    half = h2t_ref.shape[2] // 2
    nchunk = eblk // DENSE_ROWS
    per = DENSE_ROWS // PEER_NKEYS
    dq = acc_ref.shape[0] // nchunk

    def front(hf, c):
        cols = slice(hf * half, (hf + 1) * half)
        rows = slice(c * DENSE_ROWS, (c + 1) * DENSE_ROWS)
        gates = []
        for ii in range(c * per, (c + 1) * per):
            w = None
            for h in range(PEER_HEADS):
                cnt_i = cnt_ref[0, h, 0, ii:ii + 1, cols].astype(BF16)
                wa_i = wa_ref[0, h, 0, ii:ii + 1, cols].astype(BF16)
                wh = jnp.where(rb_ref[0, h, :, cols] < cnt_i, wb_ref[0, h, :, cols], jnp.zeros((), BF16)) * wa_i
                w = wh if w is None else w + wh
            gates.append(w)
        gate = jnp.concatenate(gates, axis=0)
        at = _dot(u_ref[rows, :], h2t_ref[0, :, cols])
        act = at * (1.0 + lax.erf(at * math.sqrt(0.5)))
        p_ref[rows, cols] = act.astype(BF16) * gate

    def back(hf, c):
        cols = slice(hf * half, (hf + 1) * half)
        out = slice(c * dq, (c + 1) * dq)
        acc_ref[out, cols] += _dot(vt_ref[0, out, :], p_ref[:, cols])

    for c in range(nchunk):
        front(0, c)
    for c in range(nchunk):
        front(1, c)
        back(0, c)
    for c in range(nchunk):
        back(1, c)

    @pl.when(e == pl.num_programs(2) - 1)
    def _():
        y = x_ref[0] + gt_ref[0] * acc_ref[...].T
        o_ref[0] = _rms(y, fg_ref[...]) * (1.0 + fsc_ref[0]) + fsh_ref[0]


def _dense(h2t, u, vt, rb, wb, cnt, wa, x1, gt2, fg, fsh, fsc, tm, eblk):
    bsz, s, d = x1.shape
    nblk = u.shape[0] // eblk
    gate = pl.BlockSpec((1, PEER_HEADS, PEER_NKEYS, tm), lambda b, i, e: (b, 0, 0, i))
    nrow = eblk // PEER_NKEYS
    rows = pl.BlockSpec((1, PEER_HEADS, 1, nrow, tm), lambda b, i, e: (b, 0, e, 0, i))
    by_block = lambda a: a.reshape(bsz, PEER_HEADS, nblk, nrow, s)
    tile = pl.BlockSpec((1, tm, d), lambda b, i, e: (b, i, 0))
    vec = pl.BlockSpec((1, 1, d), lambda b, i, e: (b, 0, 0))
    return pl.pallas_call(
        functools.partial(_dense_kernel, eblk=eblk),
        grid=(bsz, s // tm, nblk),
        in_specs=[pl.BlockSpec((1, d, tm), lambda b, i, e: (b, 0, i)),
                  pl.BlockSpec((eblk, d), lambda b, i, e: (e, 0)),
                  pl.BlockSpec((1, d, eblk), lambda b, i, e: (e, 0, 0)),
                  gate, gate, rows, rows, tile, vec, _resident((1, d)), vec, vec],
        out_specs=tile,
        out_shape=jax.ShapeDtypeStruct((bsz, s, d), F32),
        scratch_shapes=[pltpu.VMEM((d, tm), F32), pltpu.VMEM((eblk, tm), BF16)],
        compiler_params=_cparams("arbitrary", "arbitrary", "arbitrary"),
        name="dense",
    )(h2t, u, vt, rb, wb, by_block(cnt), by_block(wa), x1, gt2, fg.reshape(1, d), fsh, fsc)


def _tile(s, want):
    return want if s % want == 0 else s


def kernel(x, c, positions, w_ada, b_ada, norm1_g, w_in, gla_wa_fw, gla_ba_fw, gla_wa_bw, gla_ba_bw, gla_norm_g,
           diff_lq1, diff_lk1, diff_lq2, diff_lk2, diff_norm_g, w_gla_proj, w_diff_proj, w_out, norm2_g, peer_wq,
           peer_subkeys, peer_u, peer_v, w_final_ada, b_final_ada, normf_g):
    bsz, s, d = x.shape
    depth = w_ada.shape[0]
    tm = _tile(s, 512)

    half = DIFF_HD // 2
    inv = (ROPE_THETA ** (-np.arange(0, DIFF_HD, 2, dtype=np.float32) / DIFF_HD)).astype(np.float32)
    inv_row = jnp.asarray(np.tile(inv, LANES // half).reshape(1, LANES))
    inv_col = jnp.asarray(inv.reshape(half, 1))

    fmod = _ada(c, w_final_ada, b_final_ada)
    f_shift = fmod[:, :d].reshape(bsz, 1, d)
    f_scale = fmod[:, d:].reshape(bsz, 1, d)

    qk, gv_w, lr_w = 2 * GLA_HEADS * GLA_DK, GLA_HEADS * GLA_DV, 2 * GLA_LOWRANK
    o_gq, o_gk, o_gv, o_gr = 0, qk // 2, qk, qk + gv_w
    o_lr = o_gr + gv_w
    o_dq = o_lr + lr_w
    o_dk, o_dv, o_ga, o_gb = o_dq + d, o_dq + 2 * d, o_dq + 3 * d, o_dq + 4 * d

    for l in range(depth):
        lambda_init = 0.8 - 0.6 * math.exp(-0.3 * l)
        mod = _ada(c, w_ada[l], b_ada[l]).reshape(bsz, 1, N_ADA * d)
        sh1, sc1, gt1, sh2, sc2, gt2 = (mod[:, :, i * d:(i + 1) * d] for i in range(N_ADA))

        w = w_in[l]
        col = lambda c0, n: w[:, c0:c0 + n]
        wn = jnp.concatenate([col(o_gq, qk // 2), col(o_gk, qk // 2), col(o_gv, gv_w), col(o_gr, gv_w),
                              col(o_dk, d), col(o_ga, d), col(o_gb, d)], axis=1).astype(BF16)
        wlr = jnp.zeros((d, LANES), F32).at[:, :lr_w].set(col(o_lr, lr_w)).astype(BF16)
        wt = jnp.concatenate([col(o_dq, d), col(o_dv, d), col(o_gv, gv_w)], axis=1).T.astype(BF16)
        (gq, gk, gv, gr, lr, dk1, dk2, ga, gb, dqt, dvt, gvt) = _inproj(
            x, sh1, sc1, norm1_g[l].reshape(1, d), positions, wn, wlr, wt, inv_row, inv_col, tm)

        o_f, o_b = _gla(gq, gk, gv, gvt, lr, gla_wa_fw[l], gla_ba_fw[l], gla_wa_bw[l], gla_ba_bw[l], _tile(s, 512))
        o_d = _attn(dqt, dk1, dk2, dvt, diff_lq1[l], diff_lk1[l], diff_lq2[l], diff_lk2[l], lambda_init,
                    _tile(s, 2048), _tile(s, 256))
        x1 = _merge(o_f, o_b, gr, o_d, ga, gb, x, gt1, gla_norm_g[l], diff_norm_g[l],
                    w_gla_proj[l].astype(BF16), w_diff_proj[l].astype(BF16), w_out[l].astype(BF16), lambda_init, tm)

        sk = peer_subkeys[l].reshape(2 * PEER_HEADS, PEER_NKEYS, -1).astype(BF16)
        h2t, rb, wb, cnt, wa = _route(x1, sh2, sc2, norm2_g[l].reshape(1, d), peer_wq[l].T.astype(BF16), sk,
                                      _tile(s, 256))
        last = l == depth - 1
        assert last, "multi-layer stacks need the final norm split out of the expert kernel"
        eblk = 8 * PEER_NKEYS
        vt = peer_v[l].astype(BF16).reshape(-1, eblk, d).transpose(0, 2, 1)
        x = _dense(h2t, peer_u[l].astype(BF16), vt, rb, wb, cnt, wa, x1, gt2,
                   normf_g, f_shift, f_scale, tm, eblk)
    return x
```

```python
import functools
import math

import numpy as np
import jax
import jax.numpy as jnp
from jax import lax
from jax.experimental import pallas as pl
from jax.experimental.pallas import tpu as pltpu

F32 = jnp.float32
BF16 = jnp.bfloat16

RMS_EPS = 1e-6
ROPE_THETA = 10000.0
GLA_HEADS = 4
GLA_DK = 128
GLA_DV = 256
GLA_LOWRANK = 16
GLA_GATE_NORM = 16.0
GLA_CHUNK = 64
GLA_SUB = 256
DIFF_HEADS = 8
DIFF_HD = 64
DIFF_VD = 128
PEER_HEADS = 8
PEER_NKEYS = 128
PEER_TOPK = 16
N_ADA = 6

LANES = 128
VMEM_LIMIT = 48 * 1024 * 1024

NT_DIMS = (((1,), (1,)), ((), ()))


def _cparams(*sem):
    return pltpu.CompilerParams(dimension_semantics=sem, vmem_limit_bytes=VMEM_LIMIT)


def _dot(a, b):
    return jnp.dot(a, b, preferred_element_type=F32)


def _dot_nt(a, b):
    return lax.dot_general(a, b, NT_DIMS, preferred_element_type=F32)


def _rms(x, g):
    ms = jnp.mean(x * x, axis=-1, keepdims=True)
    return x * lax.rsqrt(ms + RMS_EPS) * g


def _resident(shape):
    nd = len(shape)
    return pl.BlockSpec(shape, lambda *_: (0,) * nd)


def _ada_kernel(c_ref, w_ref, b_ref, o_ref):
    c = c_ref[...]
    ca = c * jax.nn.sigmoid(c)
    o_ref[...] = jnp.dot(ca, w_ref[...], preferred_element_type=F32,
                         precision=lax.Precision.HIGHEST) + b_ref[...]


def _ada(c, w, b):
    bsz, d = c.shape
    n = w.shape[1]
    tn = 2048
    return pl.pallas_call(
        _ada_kernel,
        grid=(n // tn,),
        in_specs=[pl.BlockSpec((bsz, d), lambda j: (0, 0)),
                  pl.BlockSpec((d, tn), lambda j: (0, j)),
                  pl.BlockSpec((1, tn), lambda j: (0, j))],
        out_specs=pl.BlockSpec((bsz, tn), lambda j: (0, j)),
        out_shape=jax.ShapeDtypeStruct((bsz, n), F32),
        compiler_params=_cparams("arbitrary"),
        name="ada",
    )(c, w, b.reshape(1, n))


_C_GQ, _C_GK, _C_GV, _C_GR, _C_DK, _C_GA, _C_GB, _C_END = 0, 512, 1024, 2048, 3072, 4096, 5120, 6144
_R_DQ, _R_DV, _R_GV, _R_END = 0, 1024, 2048, 3072


def _inproj_kernel(x_ref, sh_ref, sc_ref, g_ref, pos_ref, post_ref, invr_ref, invc_ref,
                   wn_ref, wlr_ref, wt_ref,
                   gq_ref, gk_ref, gv_ref, gr_ref, lr_ref, dk1_ref, dk2_ref, ga_ref, gb_ref,
                   dqt_ref, dvt_ref, gvt_ref):
    tm = x_ref.shape[1]
    h = _rms(x_ref[0], g_ref[...]) * (1.0 + sc_ref[0]) + sh_ref[0]
    hb = h.astype(BF16)

    def proj(c0, c1):
        return _dot(hb, wn_ref[:, c0:c1])

    gq_ref[0] = (proj(_C_GQ, _C_GK) * (GLA_DK ** -0.5)).astype(BF16)
    gk_ref[0] = proj(_C_GK, _C_GV).astype(BF16)
    gv_ref[0] = proj(_C_GV, _C_GR).astype(BF16)
    gr_ref[0] = proj(_C_GR, _C_DK).astype(BF16)
    ga_ref[0] = proj(_C_GA, _C_GB).astype(BF16)
    gb_ref[0] = proj(_C_GB, _C_END).astype(BF16)
    lr_ref[0] = _dot(hb, wlr_ref[...])

    ang = pos_ref[0] * invr_ref[...]
    cs = jnp.cos(ang)
    sn = jnp.sin(ang)
    lane = lax.broadcasted_iota(jnp.int32, (tm, LANES), 1)
    first = (lane % DIFF_HD) < (DIFF_HD // 2)
    sn = jnp.where(first, -sn, sn)
    y = proj(_C_DK, _C_GA)
    pad = jnp.where(lane == DIFF_HD, 1.0, 0.0)
    for hd in range(DIFF_HEADS):
        yb = y[:, hd * LANES:(hd + 1) * LANES]
        rot = jnp.where(first, pltpu.roll(yb, LANES - DIFF_HD // 2, 1), pltpu.roll(yb, DIFF_HD // 2, 1))
        kr = yb * cs + rot * sn
        dk1_ref[0, :, hd * LANES:(hd + 1) * LANES] = jnp.where(lane < DIFF_HD, kr, pad).astype(BF16)
        dk2_ref[0, :, hd * LANES:(hd + 1) * LANES] = jnp.where(lane < DIFF_HD, pltpu.roll(kr, DIFF_HD, 1),
                                                                pad).astype(BF16)

    angt = invc_ref[...] * post_ref[0]
    ct = jnp.cos(angt)
    st = jnp.sin(angt)
    half = DIFF_HD // 2
    yt = _dot_nt(wt_ref[_R_DQ:_R_DV, :], hb)
    qscale = DIFF_HD ** -0.5 * math.log2(math.e)
    for blk in range(2 * DIFF_HEADS):
        r0 = blk * DIFF_HD
        x1 = yt[r0:r0 + half]
        x2 = yt[r0 + half:r0 + DIFF_HD]
        dqt_ref[0, r0:r0 + half, :] = ((x1 * ct - x2 * st) * qscale).astype(BF16)
        dqt_ref[0, r0 + half:r0 + DIFF_HD, :] = ((x2 * ct + x1 * st) * qscale).astype(BF16)
    dvt_ref[0] = _dot_nt(wt_ref[_R_DV:_R_GV, :], hb).astype(BF16)
    gvt_ref[0] = _dot_nt(wt_ref[_R_GV:_R_END, :], hb).astype(BF16)


def _inproj(x, sh, sc, g, pos, wn, wlr, wt, inv_row, inv_col, tm):
    bsz, s, d = x.shape
    posf = pos.astype(F32)
    nat = lambda n: pl.BlockSpec((1, tm, n), lambda b, i: (b, i, 0))
    tr = lambda n: pl.BlockSpec((1, n, tm), lambda b, i: (b, 0, i))
    vec = pl.BlockSpec((1, 1, d), lambda b, i: (b, 0, 0))
    shp = lambda *dims: jax.ShapeDtypeStruct(dims, BF16)
    return pl.pallas_call(
        _inproj_kernel,
        grid=(bsz, s // tm),
        in_specs=[nat(d), vec, vec, _resident((1, d)),
                  pl.BlockSpec((1, tm, 1), lambda b, i: (b, i, 0)),
                  pl.BlockSpec((1, 1, tm), lambda b, i: (b, 0, i)),
                  _resident(inv_row.shape), _resident(inv_col.shape),
                  _resident(wn.shape), _resident(wlr.shape), _resident(wt.shape)],
        out_specs=[nat(512), nat(512), nat(1024), nat(1024), nat(LANES), nat(1024), nat(1024), nat(1024), nat(1024),
                   tr(1024), tr(1024), tr(1024)],
        out_shape=[shp(bsz, s, 512), shp(bsz, s, 512), shp(bsz, s, 1024), shp(bsz, s, 1024),
                   jax.ShapeDtypeStruct((bsz, s, LANES), F32),
                   shp(bsz, s, 1024), shp(bsz, s, 1024), shp(bsz, s, 1024), shp(bsz, s, 1024),
                   shp(bsz, 1024, s), shp(bsz, 1024, s), shp(bsz, 1024, s)],
        compiler_params=_cparams("arbitrary", "arbitrary"),
        name="inproj",
    )(x, sh, sc, g, posf.reshape(bsz, s, 1), posf.reshape(bsz, 1, s), inv_row, inv_col, wn, wlr, wt)


def _gla_direction(q_ref, k_ref, v_ref, vt_ref, lr_ref, wh_ref, wl_ref, ba_ref, tri_ref, o_ref, s_ref, reverse):
    blk = q_ref.shape[1]
    lr = lr_ref[0]
    lr_hi = lr.astype(BF16)
    lr_lo = (lr - lr_hi.astype(F32)).astype(BF16)
    z = _dot(lr_hi, wh_ref[...]) + _dot(lr_lo, wh_ref[...]) + _dot(lr_hi, wl_ref[...]) + ba_ref[...]
    logg = -(jnp.maximum(-z, 0.0) + jnp.log1p(jnp.exp(-jnp.abs(z)))) * (1.0 / GLA_GATE_NORM)
    lg_hi = logg.astype(BF16)
    lg_lo = (logg - lg_hi.astype(F32)).astype(BF16)
    yield
    tri = tri_ref[...]
    sub = tri.shape[0]
    q = q_ref[0].astype(F32)
    k = k_ref[0].astype(F32)
    v = v_ref[0]
    b, qe, o_intra = [], [], []
    for r0 in range(0, blk, sub):
        rows = slice(r0, r0 + sub)
        b_s = _dot(tri, lg_hi[rows]) + _dot(tri, lg_lo[rows])
        qe_s = (q[rows] * jnp.exp(b_s)).astype(BF16)
        ke_s = (k[rows] * jnp.exp(-b_s)).astype(BF16)
        att = jnp.where(tri > 0, _dot_nt(qe_s, ke_s), 0.0).astype(BF16)
        b.append(b_s)
        qe.append(qe_s)
        o_intra.append(_dot(att, v[rows]))
        yield
    b = jnp.concatenate(b, axis=0)
    qe = jnp.concatenate(qe, axis=0)
    o_intra = jnp.concatenate(o_intra, axis=0)

    state = s_ref[...]
    nchunk = blk // GLA_CHUNK
    order = range(nchunk - 1, -1, -1) if reverse else range(nchunk)
    for n in order:
        r0 = n * GLA_CHUNK
        edge = r0 if reverse else r0 + GLA_CHUNK - 1
        b_c = b[r0:r0 + GLA_CHUNK]
        b_e = b[edge:edge + 1]
        kend = (k[r0:r0 + GLA_CHUNK] * jnp.exp(b_e - b_c)).astype(BF16)
        o_inter = _dot_nt(qe[r0:r0 + GLA_CHUNK], state.astype(BF16))
        o_ref[0, r0:r0 + GLA_CHUNK, :] = (o_intra[r0:r0 + GLA_CHUNK] + o_inter).astype(BF16)
        state = state * jnp.exp(b_e) + _dot(vt_ref[0, :, r0:r0 + GLA_CHUNK], kend)
        yield
    s_ref[...] = state


def _gla_kernel(qf, kf, vf, vtf, lrf, qb, kb, vb, vtb, lrb,
                wfh, wfl, baf, wbh, wbl, bab, trif, trib,
                of_ref, ob_ref, sf_ref, sb_ref):
    @pl.when(pl.program_id(2) == 0)
    def _():
        sf_ref[...] = jnp.zeros_like(sf_ref)
        sb_ref[...] = jnp.zeros_like(sb_ref)

    live = [_gla_direction(qf, kf, vf, vtf, lrf, wfh, wfl, baf, trif, of_ref, sf_ref, reverse=False),
            _gla_direction(qb, kb, vb, vtb, lrb, wbh, wbl, bab, trib, ob_ref, sb_ref, reverse=True)]
    while live:
        live = [d for d in live if next(d, True) is None]


def _gla(gq, gk, gv, gvt, lr, wa_f, ba_f, wa_b, ba_b, blk):
    bsz, s, _ = gq.shape
    nblk = s // blk
    def pad_rows(w, r0):
        return jnp.zeros((LANES, w.shape[1]), F32).at[r0:r0 + GLA_LOWRANK].set(w)

    def hi_lo(w):
        hi = w.astype(BF16)
        return hi, (w - hi.astype(F32)).astype(BF16)

    wfh, wfl = hi_lo(pad_rows(wa_f, 0))
    wbh, wbl = hi_lo(pad_rows(wa_b, GLA_LOWRANK))
    sub = min(blk, GLA_SUB)
    r = np.arange(sub)
    same = (r[:, None] // GLA_CHUNK) == (r[None, :] // GLA_CHUNK)
    tri_f = jnp.asarray(same & (r[None, :] <= r[:, None]), BF16)
    tri_b = jnp.asarray(same & (r[None, :] >= r[:, None]), BF16)

    fw = lambda b, h, i: (b, i, h)
    bw = lambda b, h, i: (b, nblk - 1 - i, h)
    fwt = lambda b, h, i: (b, h, i)
    bwt = lambda b, h, i: (b, h, nblk - 1 - i)

    def seq_specs(tok, tok_t):
        return [pl.BlockSpec((1, blk, GLA_DK), tok), pl.BlockSpec((1, blk, GLA_DK), tok),
                pl.BlockSpec((1, blk, GLA_DV), tok), pl.BlockSpec((1, GLA_DV, blk), tok_t),
                pl.BlockSpec((1, blk, LANES), lambda b, h, i, t=tok: (t(b, h, i)[0], t(b, h, i)[1], 0))]

    wspec = pl.BlockSpec((LANES, GLA_DK), lambda b, h, i: (0, h))
    bspec = pl.BlockSpec((1, GLA_DK), lambda b, h, i: (0, h))
    out_f = pl.BlockSpec((1, blk, GLA_DV), fw)
    out_b = pl.BlockSpec((1, blk, GLA_DV), bw)
    o_shape = jax.ShapeDtypeStruct((bsz, s, GLA_HEADS * GLA_DV), BF16)
    return pl.pallas_call(
        _gla_kernel,
        grid=(bsz, GLA_HEADS, nblk),
        in_specs=seq_specs(fw, fwt) + seq_specs(bw, bwt) + [wspec, wspec, bspec, wspec, wspec, bspec,
                                                             _resident((sub, sub)), _resident((sub, sub))],
        out_specs=[out_f, out_b],
        out_shape=[o_shape, o_shape],
        scratch_shapes=[pltpu.VMEM((GLA_DV, GLA_DK), F32), pltpu.VMEM((GLA_DV, GLA_DK), F32)],
        compiler_params=_cparams("arbitrary", "arbitrary", "arbitrary"),
        name="gla",
    )(gq, gk, gv, gvt, lr, gq, gk, gv, gvt, lr,
      wfh, wfl, ba_f.reshape(1, -1), wbh, wbl, ba_b.reshape(1, -1), tri_f, tri_b)


ATTN_SAFE_SHIFT = 40.0
ATTN_NORM_CHUNK = 1024


def _ring(nk, nslot, stages):
    assert nk % nslot == 0
    offs = [off for _, off in stages]

    def static_iter(i):
        for fn, off in stages:
            if 0 <= i + off < nk:
                fn(i + off, (i + off) % nslot)

    def full(g):
        return all(0 <= g * nslot + u + off < nk for u in range(nslot) for off in offs)

    ngroup = nk // nslot
    interior = [g for g in range(ngroup) if full(g)]
    g_lo, g_hi = (interior[0], interior[-1] + 1) if interior else (ngroup, ngroup)
    for i in range(-max(offs), g_lo * nslot):
        static_iter(i)
    if g_hi - g_lo == 1:
        for i in range(g_lo * nslot, g_hi * nslot):
            static_iter(i)
    elif g_hi > g_lo:
        def group(g, carry):
            for u in range(nslot):
                for fn, off in stages:
                    fn(g * nslot + u + off, (u + off) % nslot)
            return carry
        lax.fori_loop(g_lo, g_hi, group, 0)
    for i in range(g_hi * nslot, nk - min(offs)):
        static_iter(i)


def _attn_kernel(lq1_ref, lk1_ref, lq2_ref, lk2_ref, qt_ref, k1_ref, k2_ref, vt_ref, o_ref,
                 s_ref, p_ref, a_ref, mt_ref, m_ref, l_ref, acc_ref, kmax_ref, *, tk, lambda_init):
    k_refs = (k1_ref, k2_ref)
    nkeys = k1_ref.shape[1]
    nk = nkeys // tk
    nslot = p_ref.shape[0]
    dist = nslot // 2
    nslot_slow = s_ref.shape[0]
    dist_slow = nslot_slow // 2
    tq = qt_ref.shape[2]
    nchunk = max(nkeys // ATTN_NORM_CHUNK, 1)
    kc = nkeys // nchunk

    @pl.when(pl.program_id(2) == 0)
    def _():
        lane = lax.broadcasted_iota(jnp.int32, (kc, LANES), 1)
        for mi in range(2):
            def chunk(i, mx):
                kk = k_refs[mi][0, pl.ds(pl.multiple_of(i * kc, kc), kc), :].astype(F32)
                kk = jnp.where(lane < DIFF_HD, kk, 0.0)
                n2 = jnp.sum(kk * kk, axis=1, keepdims=True)
                return jnp.maximum(mx, jnp.max(n2, axis=0, keepdims=True))
            kmax_ref[mi] = jnp.sqrt(lax.fori_loop(0, nchunk, chunk, jnp.zeros((1, 1), F32)))

    qt = qt_ref[0]
    row = lax.broadcasted_iota(jnp.int32, (16, tq), 0)
    q_shift, q_plain, bound = [], [], []
    for mi in range(2):
        qm = qt[mi * DIFF_HD:(mi + 1) * DIFF_HD]
        qf = qm.astype(F32)
        c = jnp.sqrt(jnp.sum(qf * qf, axis=0, keepdims=True)) * kmax_ref[mi]
        c = c.astype(BF16).astype(F32)
        shift_rows = jnp.where(row == 0, -c, 0.0).astype(BF16)
        q_shift.append(jnp.concatenate([qm, shift_rows, jnp.zeros((DIFF_HD - 16, tq), BF16)], axis=0))
        q_plain.append(jnp.concatenate([qm, jnp.zeros((DIFF_HD, tq), BF16)], axis=0))
        bound.append(jnp.max(c))
    safe = jnp.maximum(bound[0], bound[1]) <= ATTN_SAFE_SHIFT
    acc_ref[...] = jnp.zeros_like(acc_ref)
    l_ref[...] = jnp.zeros_like(l_ref)

    def key_tile(mi, t):
        return k_refs[mi][0, pl.ds(pl.multiple_of(t * tk, tk), tk), :]

    def value_tile(t):
        return vt_ref[0, :, pl.ds(pl.multiple_of(t * tk, tk), tk)]

    @pl.when(safe)
    def _():
        def probs(t, slot):
            for mi in range(2):
                p = jnp.exp2(_dot(key_tile(mi, t), q_shift[mi]))
                l_ref[mi] += jnp.sum(p, axis=0, keepdims=True)
                p_ref[slot, mi] = p.astype(BF16)

        def values(t, slot):
            vtj = value_tile(t)
            for mi in range(2):
                acc_ref[mi] += _dot(vtj, p_ref[slot, mi])

        _ring(nk, nslot, [(values, 0), (probs, dist)])

    @pl.when(jnp.logical_not(safe))
    def _():
        m_ref[...] = jnp.full(m_ref.shape, -jnp.inf, F32)

        def scores(t, slot):
            for mi in range(2):
                s = _dot(key_tile(mi, t), q_plain[mi])
                s_ref[slot, mi] = s
                mt_ref[slot, mi] = jnp.max(s, axis=0, keepdims=True)

        def softmax(t, slot):
            for mi in range(2):
                m_old = m_ref[mi]
                m_new = jnp.maximum(m_old, mt_ref[slot, mi])
                a_ref[slot, mi] = jnp.exp2(m_old - m_new)
                p = jnp.exp2(s_ref[slot, mi] - m_new)
                l_ref[mi] = a_ref[slot, mi] * l_ref[mi] + jnp.sum(p, axis=0, keepdims=True)
                p_ref[slot, mi] = p.astype(BF16)
                m_ref[mi] = m_new

        def values(t, slot):
            vtj = value_tile(t)
            for mi in range(2):
                acc_ref[mi] = acc_ref[mi] * a_ref[slot, mi] + _dot(vtj, p_ref[slot, mi])

        _ring(nk, nslot_slow, [(softmax, 0), (values, -dist_slow), (scores, dist_slow)])

    lam = (jnp.exp(jnp.sum(lq1_ref[...] * lk1_ref[...], axis=-1, keepdims=True))
           - jnp.exp(jnp.sum(lq2_ref[...] * lk2_ref[...], axis=-1, keepdims=True)) + lambda_init)
    ot = acc_ref[0] * (1.0 / l_ref[0]) - lam * (acc_ref[1] * (1.0 / l_ref[1]))
    o_ref[0] = ot.T.astype(BF16)


def _attn(dqt, dk1, dk2, dvt, lq1, lk1, lq2, lk2, lambda_init, tq, tk):
    bsz, s, _ = dk1.shape
    lspec = _resident((1, DIFF_HD))
    kspec = pl.BlockSpec((1, s, LANES), lambda b, h, i: (b, 0, h))
    return pl.pallas_call(
        functools.partial(_attn_kernel, tk=tk, lambda_init=lambda_init),
        grid=(bsz, DIFF_HEADS, s // tq),
        in_specs=[lspec, lspec, lspec, lspec,
                  pl.BlockSpec((1, 2 * DIFF_HD, tq), lambda b, h, i: (b, h, i)),
                  kspec, kspec,
                  pl.BlockSpec((1, DIFF_VD, s), lambda b, h, i: (b, h, 0))],
        out_specs=pl.BlockSpec((1, tq, DIFF_VD), lambda b, h, i: (b, i, h)),
        out_shape=jax.ShapeDtypeStruct((bsz, s, DIFF_HEADS * DIFF_VD), BF16),
        scratch_shapes=[pltpu.VMEM((2, 2, tk, tq), F32), pltpu.VMEM((4, 2, tk, tq), BF16),
                        pltpu.VMEM((4, 2, 1, tq), F32), pltpu.VMEM((4, 2, 1, tq), F32),
                        pltpu.VMEM((2, 1, tq), F32), pltpu.VMEM((2, 1, tq), F32),
                        pltpu.VMEM((2, DIFF_VD, tq), F32),
                        pltpu.VMEM((2, 1, 1), F32)],
        compiler_params=_cparams("arbitrary", "arbitrary", "arbitrary"),
        name="attn",
    )(lq1.reshape(1, -1), lk1.reshape(1, -1), lq2.reshape(1, -1), lk2.reshape(1, -1), dqt, dk1, dk2, dvt)


def _merge_kernel(of_ref, ob_ref, gr_ref, od_ref, ga_ref, gb_ref, x_ref, gt_ref, gg_ref, dg_ref,
                  wg_ref, wd_ref, wo_ref, o_ref, *, lambda_init):
    og = of_ref[0].astype(F32) + ob_ref[0].astype(F32)
    gg = gg_ref[...]
    og = jnp.concatenate(
        [_rms(og[:, h * GLA_DV:(h + 1) * GLA_DV], gg[:, h * GLA_DV:(h + 1) * GLA_DV]) for h in range(GLA_HEADS)],
        axis=-1)
    gr = gr_ref[0].astype(F32)
    y_gla = _dot((og * (gr * jax.nn.sigmoid(gr))).astype(BF16), wg_ref[...])
    od = od_ref[0].astype(F32)
    dg = dg_ref[...]
    od = jnp.concatenate(
        [_rms(od[:, h * DIFF_VD:(h + 1) * DIFF_VD], dg[:, h * DIFF_VD:(h + 1) * DIFF_VD]) for h in range(DIFF_HEADS)],
        axis=-1) * (1.0 - lambda_init)
    y_diff = _dot(od.astype(BF16), wd_ref[...])
    merged = (jax.nn.sigmoid(ga_ref[0].astype(F32)) * y_gla + jax.nn.sigmoid(gb_ref[0].astype(F32)) * y_diff)
    o_ref[0] = x_ref[0] + gt_ref[0] * _dot(merged.astype(BF16), wo_ref[...])


def _merge(o_f, o_b, gr, o_d, ga, gb, x, gt1, gla_g, diff_g, wg, wd, wo, lambda_init, tm):
    bsz, s, d = x.shape
    tile = pl.BlockSpec((1, tm, d), lambda b, i: (b, i, 0))
    vec = pl.BlockSpec((1, 1, d), lambda b, i: (b, 0, 0))
    return pl.pallas_call(
        functools.partial(_merge_kernel, lambda_init=lambda_init),
        grid=(bsz, s // tm),
        in_specs=[tile] * 7 + [vec, _resident((1, d)), _resident((1, d)),
                               _resident((d, d)), _resident((d, d)), _resident((d, d))],
        out_specs=tile,
        out_shape=jax.ShapeDtypeStruct((bsz, s, d), F32),
        compiler_params=_cparams("arbitrary", "arbitrary"),
        name="merge",
    )(o_f, o_b, gr, o_d, ga, gb, x, gt1, gla_g.reshape(1, d), diff_g.reshape(1, d), wg, wd, wo)


_CAND_ROW_COUNT = (16, 8, 5, 4, 3, 2, 2, 2)
_SUB = 8
_SENTINEL = 2.0 ** 100


def _top16(a):
    work = a
    vals = []
    for r in range(PEER_TOPK):
        m = jnp.max(work, axis=0, keepdims=True)
        vals.append(m)
        work = jnp.where(work == m, -_SENTINEL * (1.0 + r / 32.0), work)
    rank = jnp.where(work <= -_SENTINEL, (work * (-1.0 / _SENTINEL) - 1.0) * 32.0, float(PEER_TOPK))
    return jnp.concatenate(vals, axis=0), rank


def _oddeven_merge_sort_pairs(n):
    pairs = []
    p = 1
    while p < n:
        k = p
        while k >= 1:
            for j in range(k % p, n - k, 2 * k):
                for i in range(min(k, n - j - k)):
                    if (i + j) // (2 * p) == (i + j + k) // (2 * p):
                        pairs.append((i + j, i + j + k))
            k //= 2
        p *= 2
    return pairs


def _top16_values(a):
    nslab = a.shape[0] // _SUB
    assert nslab == PEER_TOPK
    slabs = [a[k * _SUB:(k + 1) * _SUB] for k in range(nslab)]

    def order(i, j):
        slabs[i], slabs[j] = jnp.maximum(slabs[i], slabs[j]), jnp.minimum(slabs[i], slabs[j])

    for i, j in _oddeven_merge_sort_pairs(nslab):
        order(i, j)
    shift = _SUB // 2
    while shift >= 1:
        other = [pltpu.roll(x, shift, 0) for x in slabs]
        slabs = [jnp.maximum(slabs[k], other[nslab - 1 - k]) for k in range(nslab)]
        d = nslab // 2
        while d >= 1:
            for k in range(nslab):
                if k & d == 0:
                    order(k, k + d)
            d //= 2
        shift //= 2
    return jnp.concatenate([x[0:1] for x in slabs], axis=0)


def _route_head(a, b):
    t = a.shape[1]
    top_a = _top16_values(a)
    top_b, rank_b = _top16(b)
    rowi = lax.broadcasted_iota(jnp.int32, (_SUB, t), 0)
    pieces = [top_b + top_a[0:1]]
    for r in range(1, _SUB):
        piece = top_b[0:_SUB] + top_a[r:r + 1]
        n = _CAND_ROW_COUNT[r]
        pieces.append(piece if n == _SUB else jnp.where(rowi < n, piece, -jnp.inf))
    pieces.append(top_a[_SUB:PEER_TOPK] + top_b[0:1])
    cand = jnp.concatenate(pieces, axis=0)
    work = cand
    thr = None
    for _ in range(PEER_TOPK):
        thr = jnp.max(work, axis=0, keepdims=True)
        work = jnp.where(work == thr, -jnp.inf, work)
    sel = cand >= thr
    smax = top_a[0:1] + top_b[0:1]
    z = jnp.sum(jnp.where(sel, jnp.exp(cand - smax), 0.0), axis=0, keepdims=True)
    self32 = sel.astype(F32)
    counts = [jnp.sum(self32[0:PEER_TOPK], axis=0, keepdims=True)]
    for r in range(1, _SUB):
        r0 = PEER_TOPK + (r - 1) * _SUB
        counts.append(jnp.sum(self32[r0:r0 + _SUB], axis=0, keepdims=True))
    tail0 = PEER_TOPK + (_SUB - 1) * _SUB
    for r in range(_SUB, PEER_TOPK):
        counts.append(self32[tail0 + r - _SUB:tail0 + r - _SUB + 1])
    cnt = jnp.zeros(a.shape, F32)
    for r in range(PEER_TOPK):
        cnt = jnp.where(a == top_a[r:r + 1], counts[r], cnt)
    wa = jnp.exp(a - top_a[0:1]) * (0.5 / z)
    wb = jnp.exp(b - top_b[0:1])
    return rank_b, wb, cnt, wa


def _route_kernel(x_ref, sh_ref, sc_ref, g_ref, wqt_ref, sk_ref, h2t_ref, rb_ref, wb_ref, cnt_ref, wa_ref):
    h2 = _rms(x_ref[0], g_ref[...]) * (1.0 + sc_ref[0]) + sh_ref[0]
    hb = h2.astype(BF16)
    h2t_ref[0] = h2.T.astype(BF16)
    qt = _dot_nt(wqt_ref[...], hb).astype(BF16)
    for h in range(PEER_HEADS):
        g0 = 2 * h * PEER_NKEYS
        a = _dot(sk_ref[2 * h], qt[g0:g0 + PEER_NKEYS])
        b = _dot(sk_ref[2 * h + 1], qt[g0 + PEER_NKEYS:g0 + 2 * PEER_NKEYS])
        rank_b, wb, cnt, wa = _route_head(a, b)
        rb_ref[0, h] = rank_b.astype(BF16)
        wb_ref[0, h] = wb.astype(BF16)
        cnt_ref[0, h] = cnt
        wa_ref[0, h] = wa


def _route(x1, sh2, sc2, g, wqt, sk, tm):
    bsz, s, d = x1.shape
    vec = pl.BlockSpec((1, 1, d), lambda b, i: (b, 0, 0))
    gate = pl.BlockSpec((1, PEER_HEADS, PEER_NKEYS, tm), lambda b, i: (b, 0, 0, i))
    gshape = lambda dt: jax.ShapeDtypeStruct((bsz, PEER_HEADS, PEER_NKEYS, s), dt)
    return pl.pallas_call(
        _route_kernel,
        grid=(bsz, s // tm),
        in_specs=[pl.BlockSpec((1, tm, d), lambda b, i: (b, i, 0)), vec, vec, _resident((1, d)),
                  _resident(wqt.shape), _resident(sk.shape)],
        out_specs=[pl.BlockSpec((1, d, tm), lambda b, i: (b, 0, i)), gate, gate, gate, gate],
        out_shape=[jax.ShapeDtypeStruct((bsz, d, s), BF16), gshape(BF16), gshape(BF16), gshape(F32), gshape(F32)],
        compiler_params=_cparams("arbitrary", "arbitrary"),
        name="route",
    )(x1, sh2, sc2, g, wqt, sk)


DENSE_ROWS = PEER_NKEYS


def _dense_kernel(h2t_ref, u_ref, vt_ref, rb_ref, wb_ref, cnt_ref, wa_ref, x_ref, gt_ref, fg_ref, fsh_ref, fsc_ref,
                  o_ref, acc_ref, p_ref, *, eblk):
    e = pl.program_id(2)

    @pl.when(e == 0)
    def _():
        acc_ref[...] = jnp.zeros_like(acc_ref)

    half = h2t_ref.shape[2] // 2
    nchunk = eblk // DENSE_ROWS
    per = DENSE_ROWS // PEER_NKEYS
    dq = acc_ref.shape[0] // nchunk

    def front(hf, c):
        cols = slice(hf * half, (hf + 1) * half)
        rows = slice(c * DENSE_ROWS, (c + 1) * DENSE_ROWS)
        gates = []
        for ii in range(c * per, (c + 1) * per):
            w = None
            for h in range(PEER_HEADS):
                cnt_i = cnt_ref[0, h, 0, ii:ii + 1, cols].astype(BF16)
                wa_i = wa_ref[0, h, 0, ii:ii + 1, cols].astype(BF16)
                wh = jnp.where(rb_ref[0, h, :, cols] < cnt_i, wb_ref[0, h, :, cols], jnp.zeros((), BF16)) * wa_i
                w = wh if w is None else w + wh
            gates.append(w)
        gate = jnp.concatenate(gates, axis=0)
        at = _dot(u_ref[rows, :], h2t_ref[0, :, cols])
        act = at * (1.0 + lax.erf(at * math.sqrt(0.5)))
        p_ref[rows, cols] = act.astype(BF16) * gate

    def back(hf, c):
        cols = slice(hf * half, (hf + 1) * half)
        out = slice(c * dq, (c + 1) * dq)
        acc_ref[out, cols] += _dot(vt_ref[0, out, :], p_ref[:, cols])

    for c in range(nchunk):
        front(0, c)
    for c in range(nchunk):
        front(1, c)
        back(0, c)
    for c in range(nchunk):
        back(1, c)

    @pl.when(e == pl.num_programs(2) - 1)
    def _():
        y = x_ref[0] + gt_ref[0] * acc_ref[...].T
        o_ref[0] = _rms(y, fg_ref[...]) * (1.0 + fsc_ref[0]) + fsh_ref[0]


def _dense(h2t, u, vt, rb, wb, cnt, wa, x1, gt2, fg, fsh, fsc, tm, eblk):
    bsz, s, d = x1.shape
    nblk = u.shape[0] // eblk
    gate = pl.BlockSpec((1, PEER_HEADS, PEER_NKEYS, tm), lambda b, i, e: (b, 0, 0, i))
    nrow = eblk // PEER_NKEYS
    rows = pl.BlockSpec((1, PEER_HEADS, 1, nrow, tm), lambda b, i, e: (b, 0, e, 0, i))
    by_block = lambda a: a.reshape(bsz, PEER_HEADS, nblk, nrow, s)
    tile = pl.BlockSpec((1, tm, d), lambda b, i, e: (b, i, 0))
    vec = pl.BlockSpec((1, 1, d), lambda b, i, e: (b, 0, 0))
    return pl.pallas_call(
        functools.partial(_dense_kernel, eblk=eblk),
        grid=(bsz, s // tm, nblk),
        in_specs=[pl.BlockSpec((1, d, tm), lambda b, i, e: (b, 0, i)),
                  pl.BlockSpec((eblk, d), lambda b, i, e: (e, 0)),
                  pl.BlockSpec((1, d, eblk), lambda b, i, e: (e, 0, 0)),
                  gate, gate, rows, rows, tile, vec, _resident((1, d)), vec, vec],
        out_specs=tile,
        out_shape=jax.ShapeDtypeStruct((bsz, s, d), F32),
        scratch_shapes=[pltpu.VMEM((d, tm), F32), pltpu.VMEM((eblk, tm), BF16)],
        compiler_params=_cparams("arbitrary", "arbitrary", "arbitrary"),
        name="dense",
    )(h2t, u, vt, rb, wb, by_block(cnt), by_block(wa), x1, gt2, fg.reshape(1, d), fsh, fsc)


def _tile(s, want):
    return want if s % want == 0 else s


def kernel(x, c, positions, w_ada, b_ada, norm1_g, w_in, gla_wa_fw, gla_ba_fw, gla_wa_bw, gla_ba_bw, gla_norm_g,
           diff_lq1, diff_lk1, diff_lq2, diff_lk2, diff_norm_g, w_gla_proj, w_diff_proj, w_out, norm2_g, peer_wq,
           peer_subkeys, peer_u, peer_v, w_final_ada, b_final_ada, normf_g):
    bsz, s, d = x.shape
    depth = w_ada.shape[0]
    tm = _tile(s, 512)

    half = DIFF_HD // 2
    inv = (ROPE_THETA ** (-np.arange(0, DIFF_HD, 2, dtype=np.float32) / DIFF_HD)).astype(np.float32)
    inv_row = jnp.asarray(np.tile(inv, LANES // half).reshape(1, LANES))
    inv_col = jnp.asarray(inv.reshape(half, 1))

    fmod = _ada(c, w_final_ada, b_final_ada)
    f_shift = fmod[:, :d].reshape(bsz, 1, d)
    f_scale = fmod[:, d:].reshape(bsz, 1, d)

    qk, gv_w, lr_w = 2 * GLA_HEADS * GLA_DK, GLA_HEADS * GLA_DV, 2 * GLA_LOWRANK
    o_gq, o_gk, o_gv, o_gr = 0, qk // 2, qk, qk + gv_w
    o_lr = o_gr + gv_w
    o_dq = o_lr + lr_w
    o_dk, o_dv, o_ga, o_gb = o_dq + d, o_dq + 2 * d, o_dq + 3 * d, o_dq + 4 * d

    for l in range(depth):
        lambda_init = 0.8 - 0.6 * math.exp(-0.3 * l)
        mod = _ada(c, w_ada[l], b_ada[l]).reshape(bsz, 1, N_ADA * d)
        sh1, sc1, gt1, sh2, sc2, gt2 = (mod[:, :, i * d:(i + 1) * d] for i in range(N_ADA))

        w = w_in[l]
        col = lambda c0, n: w[:, c0:c0 + n]
        wn = jnp.concatenate([col(o_gq, qk // 2), col(o_gk, qk // 2), col(o_gv, gv_w), col(o_gr, gv_w),
                              col(o_dk, d), col(o_ga, d), col(o_gb, d)], axis=1).astype(BF16)
        wlr = jnp.zeros((d, LANES), F32).at[:, :lr_w].set(col(o_lr, lr_w)).astype(BF16)
        wt = jnp.concatenate([col(o_dq, d), col(o_dv, d), col(o_gv, gv_w)], axis=1).T.astype(BF16)
        (gq, gk, gv, gr, lr, dk1, dk2, ga, gb, dqt, dvt, gvt) = _inproj(
            x, sh1, sc1, norm1_g[l].reshape(1, d), positions, wn, wlr, wt, inv_row, inv_col, tm)

        o_f, o_b = _gla(gq, gk, gv, gvt, lr, gla_wa_fw[l], gla_ba_fw[l], gla_wa_bw[l], gla_ba_bw[l], _tile(s, 512))
        o_d = _attn(dqt, dk1, dk2, dvt, diff_lq1[l], diff_lk1[l], diff_lq2[l], diff_lk2[l], lambda_init,
                    _tile(s, 2048), _tile(s, 256))
        x1 = _merge(o_f, o_b, gr, o_d, ga, gb, x, gt1, gla_norm_g[l], diff_norm_g[l],
                    w_gla_proj[l].astype(BF16), w_diff_proj[l].astype(BF16), w_out[l].astype(BF16), lambda_init, tm)

        sk = peer_subkeys[l].reshape(2 * PEER_HEADS, PEER_NKEYS, -1).astype(BF16)
        h2t, rb, wb, cnt, wa = _route(x1, sh2, sc2, norm2_g[l].reshape(1, d), peer_wq[l].T.astype(BF16), sk,
                                      _tile(s, 256))
        last = l == depth - 1
        assert last, "multi-layer stacks need the final norm split out of the expert kernel"
        eblk = 8 * PEER_NKEYS
        vt = peer_v[l].astype(BF16).reshape(-1, eblk, d).transpose(0, 2, 1)
        x = _dense(h2t, peer_u[l].astype(BF16), vt, rb, wb, cnt, wa, x1, gt2,
                   normf_g, f_shift, f_scale, tm, eblk)
    return x
```

```python
import functools
import math

import numpy as np
import jax
import jax.numpy as jnp
from jax import lax
from jax.experimental import pallas as pl
from jax.experimental.pallas import tpu as pltpu

F32 = jnp.float32
BF16 = jnp.bfloat16

RMS_EPS = 1e-6
ROPE_THETA = 10000.0
GLA_HEADS = 4
GLA_DK = 128
GLA_DV = 256
GLA_LOWRANK = 16
GLA_GATE_NORM = 16.0
GLA_CHUNK = 64
GLA_SUB = 256
DIFF_HEADS = 8
DIFF_HD = 64
DIFF_VD = 128
PEER_HEADS = 8
PEER_NKEYS = 128
PEER_TOPK = 16
N_ADA = 6

LANES = 128
VMEM_LIMIT = 48 * 1024 * 1024

NT_DIMS = (((1,), (1,)), ((), ()))


def _cparams(*sem):
    return pltpu.CompilerParams(dimension_semantics=sem, vmem_limit_bytes=VMEM_LIMIT)


def _dot(a, b):
    return jnp.dot(a, b, preferred_element_type=F32)


def _dot_nt(a, b):
    return lax.dot_general(a, b, NT_DIMS, preferred_element_type=F32)


def _rms(x, g):
    ms = jnp.mean(x * x, axis=-1, keepdims=True)
    return x * lax.rsqrt(ms + RMS_EPS) * g


def _resident(shape):
    nd = len(shape)
    return pl.BlockSpec(shape, lambda *_: (0,) * nd)


def _ada_kernel(c_ref, w_ref, b_ref, o_ref):
    c = c_ref[...]
    ca = c * jax.nn.sigmoid(c)
    o_ref[...] = jnp.dot(ca, w_ref[...], preferred_element_type=F32,
                         precision=lax.Precision.HIGHEST) + b_ref[...]


def _ada(c, w, b):
    bsz, d = c.shape
    n = w.shape[1]
    tn = 2048
    return pl.pallas_call(
        _ada_kernel,
        grid=(n // tn,),
        in_specs=[pl.BlockSpec((bsz, d), lambda j: (0, 0)),
                  pl.BlockSpec((d, tn), lambda j: (0, j)),
                  pl.BlockSpec((1, tn), lambda j: (0, j))],
        out_specs=pl.BlockSpec((bsz, tn), lambda j: (0, j)),
        out_shape=jax.ShapeDtypeStruct((bsz, n), F32),
        compiler_params=_cparams("arbitrary"),
        name="ada",
    )(c, w, b.reshape(1, n))


_C_GQ, _C_GK, _C_GV, _C_GR, _C_DK, _C_GA, _C_GB, _C_END = 0, 512, 1024, 2048, 3072, 4096, 5120, 6144
_R_DQ, _R_DV, _R_GV, _R_END = 0, 1024, 2048, 3072


def _inproj_kernel(x_ref, sh_ref, sc_ref, g_ref, pos_ref, post_ref, invr_ref, invc_ref,
                   wn_ref, wlr_ref, wt_ref,
                   gq_ref, gk_ref, gv_ref, gr_ref, lr_ref, dk1_ref, dk2_ref, ga_ref, gb_ref,
                   dqt_ref, dvt_ref, gvt_ref):
    tm = x_ref.shape[1]
    h = _rms(x_ref[0], g_ref[...]) * (1.0 + sc_ref[0]) + sh_ref[0]
    hb = h.astype(BF16)

    def proj(c0, c1):
        return _dot(hb, wn_ref[:, c0:c1])

    gq_ref[0] = (proj(_C_GQ, _C_GK) * (GLA_DK ** -0.5)).astype(BF16)
    gk_ref[0] = proj(_C_GK, _C_GV).astype(BF16)
    gv_ref[0] = proj(_C_GV, _C_GR).astype(BF16)
    gr_ref[0] = proj(_C_GR, _C_DK).astype(BF16)
    ga_ref[0] = proj(_C_GA, _C_GB).astype(BF16)
    gb_ref[0] = proj(_C_GB, _C_END).astype(BF16)
    lr_ref[0] = _dot(hb, wlr_ref[...])

    ang = pos_ref[0] * invr_ref[...]
    cs = jnp.cos(ang)
    sn = jnp.sin(ang)
    lane = lax.broadcasted_iota(jnp.int32, (tm, LANES), 1)
    first = (lane % DIFF_HD) < (DIFF_HD // 2)
    sn = jnp.where(first, -sn, sn)
    y = proj(_C_DK, _C_GA)
    pad = jnp.where(lane == DIFF_HD, 1.0, 0.0)
    for hd in range(DIFF_HEADS):
        yb = y[:, hd * LANES:(hd + 1) * LANES]
        rot = jnp.where(first, pltpu.roll(yb, LANES - DIFF_HD // 2, 1), pltpu.roll(yb, DIFF_HD // 2, 1))
        kr = yb * cs + rot * sn
        dk1_ref[0, :, hd * LANES:(hd + 1) * LANES] = jnp.where(lane < DIFF_HD, kr, pad).astype(BF16)
        dk2_ref[0, :, hd * LANES:(hd + 1) * LANES] = jnp.where(lane < DIFF_HD, pltpu.roll(kr, DIFF_HD, 1),
                                                                pad).astype(BF16)

    angt = invc_ref[...] * post_ref[0]
    ct = jnp.cos(angt)
    st = jnp.sin(angt)
    half = DIFF_HD // 2
    yt = _dot_nt(wt_ref[_R_DQ:_R_DV, :], hb)
    qscale = DIFF_HD ** -0.5 * math.log2(math.e)
    for blk in range(2 * DIFF_HEADS):
        r0 = blk * DIFF_HD
        x1 = yt[r0:r0 + half]
        x2 = yt[r0 + half:r0 + DIFF_HD]
        dqt_ref[0, r0:r0 + half, :] = ((x1 * ct - x2 * st) * qscale).astype(BF16)
        dqt_ref[0, r0 + half:r0 + DIFF_HD, :] = ((x2 * ct + x1 * st) * qscale).astype(BF16)
    dvt_ref[0] = _dot_nt(wt_ref[_R_DV:_R_GV, :], hb).astype(BF16)
    gvt_ref[0] = _dot_nt(wt_ref[_R_GV:_R_END, :], hb).astype(BF16)


def _inproj(x, sh, sc, g, pos, wn, wlr, wt, inv_row, inv_col, tm):
    bsz, s, d = x.shape
    posf = pos.astype(F32)
    nat = lambda n: pl.BlockSpec((1, tm, n), lambda b, i: (b, i, 0))
    tr = lambda n: pl.BlockSpec((1, n, tm), lambda b, i: (b, 0, i))
    vec = pl.BlockSpec((1, 1, d), lambda b, i: (b, 0, 0))
    shp = lambda *dims: jax.ShapeDtypeStruct(dims, BF16)
    return pl.pallas_call(
        _inproj_kernel,
        grid=(bsz, s // tm),
        in_specs=[nat(d), vec, vec, _resident((1, d)),
                  pl.BlockSpec((1, tm, 1), lambda b, i: (b, i, 0)),
                  pl.BlockSpec((1, 1, tm), lambda b, i: (b, 0, i)),
                  _resident(inv_row.shape), _resident(inv_col.shape),
                  _resident(wn.shape), _resident(wlr.shape), _resident(wt.shape)],
        out_specs=[nat(512), nat(512), nat(1024), nat(1024), nat(LANES), nat(1024), nat(1024), nat(1024), nat(1024),
                   tr(1024), tr(1024), tr(1024)],
        out_shape=[shp(bsz, s, 512), shp(bsz, s, 512), shp(bsz, s, 1024), shp(bsz, s, 1024),
                   jax.ShapeDtypeStruct((bsz, s, LANES), F32),
                   shp(bsz, s, 1024), shp(bsz, s, 1024), shp(bsz, s, 1024), shp(bsz, s, 1024),
                   shp(bsz, 1024, s), shp(bsz, 1024, s), shp(bsz, 1024, s)],
        compiler_params=_cparams("arbitrary", "arbitrary"),
        name="inproj",
    )(x, sh, sc, g, posf.reshape(bsz, s, 1), posf.reshape(bsz, 1, s), inv_row, inv_col, wn, wlr, wt)


def _gla_direction(q_ref, k_ref, v_ref, vt_ref, lr_ref, wh_ref, wl_ref, ba_ref, tri_ref, o_ref, s_ref, reverse):
    blk = q_ref.shape[1]
    lr = lr_ref[0]
    lr_hi = lr.astype(BF16)
    lr_lo = (lr - lr_hi.astype(F32)).astype(BF16)
    z = _dot(lr_hi, wh_ref[...]) + _dot(lr_lo, wh_ref[...]) + _dot(lr_hi, wl_ref[...]) + ba_ref[...]
    logg = -(jnp.maximum(-z, 0.0) + jnp.log1p(jnp.exp(-jnp.abs(z)))) * (1.0 / GLA_GATE_NORM)
    lg_hi = logg.astype(BF16)
    lg_lo = (logg - lg_hi.astype(F32)).astype(BF16)
    yield
    tri = tri_ref[...]
    sub = tri.shape[0]
    q = q_ref[0].astype(F32)
    k = k_ref[0].astype(F32)
    v = v_ref[0]
    b, qe, o_intra = [], [], []
    for r0 in range(0, blk, sub):
        rows = slice(r0, r0 + sub)
        b_s = _dot(tri, lg_hi[rows]) + _dot(tri, lg_lo[rows])
        qe_s = (q[rows] * jnp.exp(b_s)).astype(BF16)
        ke_s = (k[rows] * jnp.exp(-b_s)).astype(BF16)
        att = jnp.where(tri > 0, _dot_nt(qe_s, ke_s), 0.0).astype(BF16)
        b.append(b_s)
        qe.append(qe_s)
        o_intra.append(_dot(att, v[rows]))
        yield
    b = jnp.concatenate(b, axis=0)
    qe = jnp.concatenate(qe, axis=0)
    o_intra = jnp.concatenate(o_intra, axis=0)

    state = s_ref[...]
    nchunk = blk // GLA_CHUNK
    order = range(nchunk - 1, -1, -1) if reverse else range(nchunk)
    for n in order:
        r0 = n * GLA_CHUNK
        edge = r0 if reverse else r0 + GLA_CHUNK - 1
        b_c = b[r0:r0 + GLA_CHUNK]
        b_e = b[edge:edge + 1]
        kend = (k[r0:r0 + GLA_CHUNK] * jnp.exp(b_e - b_c)).astype(BF16)
        o_inter = _dot_nt(qe[r0:r0 + GLA_CHUNK], state.astype(BF16))
        o_ref[0, r0:r0 + GLA_CHUNK, :] = (o_intra[r0:r0 + GLA_CHUNK] + o_inter).astype(BF16)
        state = state * jnp.exp(b_e) + _dot(vt_ref[0, :, r0:r0 + GLA_CHUNK], kend)
        yield
    s_ref[...] = state


def _gla_kernel(qf, kf, vf, vtf, lrf, qb, kb, vb, vtb, lrb,
                wfh, wfl, baf, wbh, wbl, bab, trif, trib,
                of_ref, ob_ref, sf_ref, sb_ref):
    @pl.when(pl.program_id(2) == 0)
    def _():
        sf_ref[...] = jnp.zeros_like(sf_ref)
        sb_ref[...] = jnp.zeros_like(sb_ref)

    live = [_gla_direction(qf, kf, vf, vtf, lrf, wfh, wfl, baf, trif, of_ref, sf_ref, reverse=False),
            _gla_direction(qb, kb, vb, vtb, lrb, wbh, wbl, bab, trib, ob_ref, sb_ref, reverse=True)]
    while live:
        live = [d for d in live if next(d, True) is None]


def _gla(gq, gk, gv, gvt, lr, wa_f, ba_f, wa_b, ba_b, blk):
    bsz, s, _ = gq.shape
    nblk = s // blk
    def pad_rows(w, r0):
        return jnp.zeros((LANES, w.shape[1]), F32).at[r0:r0 + GLA_LOWRANK].set(w)

    def hi_lo(w):
        hi = w.astype(BF16)
        return hi, (w - hi.astype(F32)).astype(BF16)

    wfh, wfl = hi_lo(pad_rows(wa_f, 0))
    wbh, wbl = hi_lo(pad_rows(wa_b, GLA_LOWRANK))
    sub = min(blk, GLA_SUB)
    r = np.arange(sub)
    same = (r[:, None] // GLA_CHUNK) == (r[None, :] // GLA_CHUNK)
    tri_f = jnp.asarray(same & (r[None, :] <= r[:, None]), BF16)
    tri_b = jnp.asarray(same & (r[None, :] >= r[:, None]), BF16)

    fw = lambda b, h, i: (b, i, h)
    bw = lambda b, h, i: (b, nblk - 1 - i, h)
    fwt = lambda b, h, i: (b, h, i)
    bwt = lambda b, h, i: (b, h, nblk - 1 - i)

    def seq_specs(tok, tok_t):
        return [pl.BlockSpec((1, blk, GLA_DK), tok), pl.BlockSpec((1, blk, GLA_DK), tok),
                pl.BlockSpec((1, blk, GLA_DV), tok), pl.BlockSpec((1, GLA_DV, blk), tok_t),
                pl.BlockSpec((1, blk, LANES), lambda b, h, i, t=tok: (t(b, h, i)[0], t(b, h, i)[1], 0))]

    wspec = pl.BlockSpec((LANES, GLA_DK), lambda b, h, i: (0, h))
    bspec = pl.BlockSpec((1, GLA_DK), lambda b, h, i: (0, h))
    out_f = pl.BlockSpec((1, blk, GLA_DV), fw)
    out_b = pl.BlockSpec((1, blk, GLA_DV), bw)
    o_shape = jax.ShapeDtypeStruct((bsz, s, GLA_HEADS * GLA_DV), BF16)
    return pl.pallas_call(
        _gla_kernel,
        grid=(bsz, GLA_HEADS, nblk),
        in_specs=seq_specs(fw, fwt) + seq_specs(bw, bwt) + [wspec, wspec, bspec, wspec, wspec, bspec,
                                                             _resident((sub, sub)), _resident((sub, sub))],
        out_specs=[out_f, out_b],
        out_shape=[o_shape, o_shape],
        scratch_shapes=[pltpu.VMEM((GLA_DV, GLA_DK), F32), pltpu.VMEM((GLA_DV, GLA_DK), F32)],
        compiler_params=_cparams("arbitrary", "arbitrary", "arbitrary"),
        name="gla",
    )(gq, gk, gv, gvt, lr, gq, gk, gv, gvt, lr,
      wfh, wfl, ba_f.reshape(1, -1), wbh, wbl, ba_b.reshape(1, -1), tri_f, tri_b)


ATTN_SAFE_SHIFT = 40.0
ATTN_NORM_CHUNK = 1024


def _ring(nk, nslot, stages):
    assert nk % nslot == 0
    offs = [off for _, off in stages]

    def static_iter(i):
        for fn, off in stages:
            if 0 <= i + off < nk:
                fn(i + off, (i + off) % nslot)

    def full(g):
        return all(0 <= g * nslot + u + off < nk for u in range(nslot) for off in offs)

    ngroup = nk // nslot
    interior = [g for g in range(ngroup) if full(g)]
    g_lo, g_hi = (interior[0], interior[-1] + 1) if interior else (ngroup, ngroup)
    for i in range(-max(offs), g_lo * nslot):
        static_iter(i)
    if g_hi - g_lo == 1:
        for i in range(g_lo * nslot, g_hi * nslot):
            static_iter(i)
    elif g_hi > g_lo:
        def group(g, carry):
            for u in range(nslot):
                for fn, off in stages:
                    fn(g * nslot + u + off, (u + off) % nslot)
            return carry
        lax.fori_loop(g_lo, g_hi, group, 0)
    for i in range(g_hi * nslot, nk - min(offs)):
        static_iter(i)


def _attn_kernel(lq1_ref, lk1_ref, lq2_ref, lk2_ref, qt_ref, k1_ref, k2_ref, vt_ref, o_ref,
                 s_ref, p_ref, a_ref, mt_ref, m_ref, l_ref, acc_ref, kmax_ref, *, tk, lambda_init):
    k_refs = (k1_ref, k2_ref)
    nkeys = k1_ref.shape[1]
    nk = nkeys // tk
    nslot = p_ref.shape[0]
    dist = nslot // 2
    nslot_slow = s_ref.shape[0]
    dist_slow = nslot_slow // 2
    tq = qt_ref.shape[2]
    nchunk = max(nkeys // ATTN_NORM_CHUNK, 1)
    kc = nkeys // nchunk

    @pl.when(pl.program_id(2) == 0)
    def _():
        lane = lax.broadcasted_iota(jnp.int32, (kc, LANES), 1)
        for mi in range(2):
            def chunk(i, mx):
                kk = k_refs[mi][0, pl.ds(pl.multiple_of(i * kc, kc), kc), :].astype(F32)
                kk = jnp.where(lane < DIFF_HD, kk, 0.0)
                n2 = jnp.sum(kk * kk, axis=1, keepdims=True)
                return jnp.maximum(mx, jnp.max(n2, axis=0, keepdims=True))
            kmax_ref[mi] = jnp.sqrt(lax.fori_loop(0, nchunk, chunk, jnp.zeros((1, 1), F32)))

    qt = qt_ref[0]
    row = lax.broadcasted_iota(jnp.int32, (16, tq), 0)
    q_shift, q_plain, bound = [], [], []
    for mi in range(2):
        qm = qt[mi * DIFF_HD:(mi + 1) * DIFF_HD]
        qf = qm.astype(F32)
        c = jnp.sqrt(jnp.sum(qf * qf, axis=0, keepdims=True)) * kmax_ref[mi]
        c = c.astype(BF16).astype(F32)
        shift_rows = jnp.where(row == 0, -c, 0.0).astype(BF16)
        q_shift.append(jnp.concatenate([qm, shift_rows, jnp.zeros((DIFF_HD - 16, tq), BF16)], axis=0))
        q_plain.append(jnp.concatenate([qm, jnp.zeros((DIFF_HD, tq), BF16)], axis=0))
        bound.append(jnp.max(c))
    safe = jnp.maximum(bound[0], bound[1]) <= ATTN_SAFE_SHIFT
    acc_ref[...] = jnp.zeros_like(acc_ref)
    l_ref[...] = jnp.zeros_like(l_ref)

    def key_tile(mi, t):
        return k_refs[mi][0, pl.ds(pl.multiple_of(t * tk, tk), tk), :]

    def value_tile(t):
        return vt_ref[0, :, pl.ds(pl.multiple_of(t * tk, tk), tk)]

    @pl.when(safe)
    def _():
        def probs(t, slot):
            for mi in range(2):
                p = jnp.exp2(_dot(key_tile(mi, t), q_shift[mi]))
                l_ref[mi] += jnp.sum(p, axis=0, keepdims=True)
                p_ref[slot, mi] = p.astype(BF16)

        def values(t, slot):
            vtj = value_tile(t)
            for mi in range(2):
                acc_ref[mi] += _dot(vtj, p_ref[slot, mi])

        _ring(nk, nslot, [(values, 0), (probs, dist)])

    @pl.when(jnp.logical_not(safe))
    def _():
        m_ref[...] = jnp.full(m_ref.shape, -jnp.inf, F32)

        def scores(t, slot):
            for mi in range(2):
                s = _dot(key_tile(mi, t), q_plain[mi])
                s_ref[slot, mi] = s
                mt_ref[slot, mi] = jnp.max(s, axis=0, keepdims=True)

        def softmax(t, slot):
            for mi in range(2):
                m_old = m_ref[mi]
                m_new = jnp.maximum(m_old, mt_ref[slot, mi])
                a_ref[slot, mi] = jnp.exp2(m_old - m_new)
                p = jnp.exp2(s_ref[slot, mi] - m_new)
                l_ref[mi] = a_ref[slot, mi] * l_ref[mi] + jnp.sum(p, axis=0, keepdims=True)
                p_ref[slot, mi] = p.astype(BF16)
                m_ref[mi] = m_new

        def values(t, slot):
            vtj = value_tile(t)
            for mi in range(2):
                acc_ref[mi] = acc_ref[mi] * a_ref[slot, mi] + _dot(vtj, p_ref[slot, mi])

        _ring(nk, nslot_slow, [(softmax, 0), (values, -dist_slow), (scores, dist_slow)])

    lam = (jnp.exp(jnp.sum(lq1_ref[...] * lk1_ref[...], axis=-1, keepdims=True))
           - jnp.exp(jnp.sum(lq2_ref[...] * lk2_ref[...], axis=-1, keepdims=True)) + lambda_init)
    ot = acc_ref[0] * (1.0 / l_ref[0]) - lam * (acc_ref[1] * (1.0 / l_ref[1]))
    o_ref[0] = ot.T.astype(BF16)


def _attn(dqt, dk1, dk2, dvt, lq1, lk1, lq2, lk2, lambda_init, tq, tk):
    bsz, s, _ = dk1.shape
    lspec = _resident((1, DIFF_HD))
    kspec = pl.BlockSpec((1, s, LANES), lambda b, h, i: (b, 0, h))
    return pl.pallas_call(
        functools.partial(_attn_kernel, tk=tk, lambda_init=lambda_init),
        grid=(bsz, DIFF_HEADS, s // tq),
        in_specs=[lspec, lspec, lspec, lspec,
                  pl.BlockSpec((1, 2 * DIFF_HD, tq), lambda b, h, i: (b, h, i)),
                  kspec, kspec,
                  pl.BlockSpec((1, DIFF_VD, s), lambda b, h, i: (b, h, 0))],
        out_specs=pl.BlockSpec((1, tq, DIFF_VD), lambda b, h, i: (b, i, h)),
        out_shape=jax.ShapeDtypeStruct((bsz, s, DIFF_HEADS * DIFF_VD), BF16),
        scratch_shapes=[pltpu.VMEM((2, 2, tk, tq), F32), pltpu.VMEM((4, 2, tk, tq), BF16),
                        pltpu.VMEM((4, 2, 1, tq), F32), pltpu.VMEM((4, 2, 1, tq), F32),
                        pltpu.VMEM((2, 1, tq), F32), pltpu.VMEM((2, 1, tq), F32),
                        pltpu.VMEM((2, DIFF_VD, tq), F32),
                        pltpu.VMEM((2, 1, 1), F32)],
        compiler_params=_cparams("arbitrary", "arbitrary", "arbitrary"),
        name="attn",
    )(lq1.reshape(1, -1), lk1.reshape(1, -1), lq2.reshape(1, -1), lk2.reshape(1, -1), dqt, dk1, dk2, dvt)


def _merge_kernel(of_ref, ob_ref, gr_ref, od_ref, ga_ref, gb_ref, x_ref, gt_ref, gg_ref, dg_ref,
                  wg_ref, wd_ref, wo_ref, o_ref, *, lambda_init):
    og = of_ref[0].astype(F32) + ob_ref[0].astype(F32)
    gg = gg_ref[...]
    og = jnp.concatenate(
        [_rms(og[:, h * GLA_DV:(h + 1) * GLA_DV], gg[:, h * GLA_DV:(h + 1) * GLA_DV]) for h in range(GLA_HEADS)],
        axis=-1)
    gr = gr_ref[0].astype(F32)
    y_gla = _dot((og * (gr * jax.nn.sigmoid(gr))).astype(BF16), wg_ref[...])
    od = od_ref[0].astype(F32)
    dg = dg_ref[...]
    od = jnp.concatenate(
        [_rms(od[:, h * DIFF_VD:(h + 1) * DIFF_VD], dg[:, h * DIFF_VD:(h + 1) * DIFF_VD]) for h in range(DIFF_HEADS)],
        axis=-1) * (1.0 - lambda_init)
    y_diff = _dot(od.astype(BF16), wd_ref[...])
    merged = (jax.nn.sigmoid(ga_ref[0].astype(F32)) * y_gla + jax.nn.sigmoid(gb_ref[0].astype(F32)) * y_diff)
    o_ref[0] = x_ref[0] + gt_ref[0] * _dot(merged.astype(BF16), wo_ref[...])


def _merge(o_f, o_b, gr, o_d, ga, gb, x, gt1, gla_g, diff_g, wg, wd, wo, lambda_init, tm):
    bsz, s, d = x.shape
    tile = pl.BlockSpec((1, tm, d), lambda b, i: (b, i, 0))
    vec = pl.BlockSpec((1, 1, d), lambda b, i: (b, 0, 0))
    return pl.pallas_call(
        functools.partial(_merge_kernel, lambda_init=lambda_init),
        grid=(bsz, s // tm),
        in_specs=[tile] * 7 + [vec, _resident((1, d)), _resident((1, d)),
                               _resident((d, d)), _resident((d, d)), _resident((d, d))],
        out_specs=tile,
        out_shape=jax.ShapeDtypeStruct((bsz, s, d), F32),
        compiler_params=_cparams("arbitrary", "arbitrary"),
        name="merge",
    )(o_f, o_b, gr, o_d, ga, gb, x, gt1, gla_g.reshape(1, d), diff_g.reshape(1, d), wg, wd, wo)


_CAND_ROW_COUNT = (16, 8, 5, 4, 3, 2, 2, 2)
_SUB = 8
_SENTINEL = 2.0 ** 100


def _top16(a):
    work = a
    vals = []
    for r in range(PEER_TOPK):
        m = jnp.max(work, axis=0, keepdims=True)
        vals.append(m)
        work = jnp.where(work == m, -_SENTINEL * (1.0 + r / 32.0), work)
    rank = jnp.where(work <= -_SENTINEL, (work * (-1.0 / _SENTINEL) - 1.0) * 32.0, float(PEER_TOPK))
    return jnp.concatenate(vals, axis=0), rank


def _oddeven_merge_sort_pairs(n):
    pairs = []
    p = 1
    while p < n:
        k = p
        while k >= 1:
            for j in range(k % p, n - k, 2 * k):
                for i in range(min(k, n - j - k)):
                    if (i + j) // (2 * p) == (i + j + k) // (2 * p):
                        pairs.append((i + j, i + j + k))
            k //= 2
        p *= 2
    return pairs


def _top16_values(a):
    nslab = a.shape[0] // _SUB
    assert nslab == PEER_TOPK
    slabs = [a[k * _SUB:(k + 1) * _SUB] for k in range(nslab)]

    def order(i, j):
        slabs[i], slabs[j] = jnp.maximum(slabs[i], slabs[j]), jnp.minimum(slabs[i], slabs[j])

    for i, j in _oddeven_merge_sort_pairs(nslab):
        order(i, j)
    shift = _SUB // 2
    while shift >= 1:
        other = [pltpu.roll(x, shift, 0) for x in slabs]
        slabs = [jnp.maximum(slabs[k], other[nslab - 1 - k]) for k in range(nslab)]
        d = nslab // 2
        while d >= 1:
            for k in range(nslab):
                if k & d == 0:
                    order(k, k + d)
            d //= 2
        shift //= 2
    return jnp.concatenate([x[0:1] for x in slabs], axis=0)


def _route_head(a, b):
    t = a.shape[1]
    top_a = _top16_values(a)
    top_b, rank_b = _top16(b)
    rowi = lax.broadcasted_iota(jnp.int32, (_SUB, t), 0)
    pieces = [top_b + top_a[0:1]]
    for r in range(1, _SUB):
        piece = top_b[0:_SUB] + top_a[r:r + 1]
        n = _CAND_ROW_COUNT[r]
        pieces.append(piece if n == _SUB else jnp.where(rowi < n, piece, -jnp.inf))
    pieces.append(top_a[_SUB:PEER_TOPK] + top_b[0:1])
    cand = jnp.concatenate(pieces, axis=0)
    work = cand
    thr = None
    for _ in range(PEER_TOPK):
        thr = jnp.max(work, axis=0, keepdims=True)
        work = jnp.where(work == thr, -jnp.inf, work)
    sel = cand >= thr
    smax = top_a[0:1] + top_b[0:1]
    z = jnp.sum(jnp.where(sel, jnp.exp(cand - smax), 0.0), axis=0, keepdims=True)
    self32 = sel.astype(F32)
    counts = [jnp.sum(self32[0:PEER_TOPK], axis=0, keepdims=True)]
    for r in range(1, _SUB):
        r0 = PEER_TOPK + (r - 1) * _SUB
        counts.append(jnp.sum(self32[r0:r0 + _SUB], axis=0, keepdims=True))
    tail0 = PEER_TOPK + (_SUB - 1) * _SUB
    for r in range(_SUB, PEER_TOPK):
        counts.append(self32[tail0 + r - _SUB:tail0 + r - _SUB + 1])
    cnt = jnp.zeros(a.shape, F32)
    for r in range(PEER_TOPK):
        cnt = jnp.where(a == top_a[r:r + 1], counts[r], cnt)
    wa = jnp.exp(a - top_a[0:1]) * (0.5 / z)
    wb = jnp.exp(b - top_b[0:1])
    return rank_b, wb, cnt, wa


def _route_kernel(x_ref, sh_ref, sc_ref, g_ref, wqt_ref, sk_ref, h2t_ref, rb_ref, wb_ref, cnt_ref, wa_ref):
    h2 = _rms(x_ref[0], g_ref[...]) * (1.0 + sc_ref[0]) + sh_ref[0]
    hb = h2.astype(BF16)
    h2t_ref[0] = h2.T.astype(BF16)
    qt = _dot_nt(wqt_ref[...], hb).astype(BF16)
    for h in range(PEER_HEADS):
        g0 = 2 * h * PEER_NKEYS
        a = _dot(sk_ref[2 * h], qt[g0:g0 + PEER_NKEYS])
        b = _dot(sk_ref[2 * h + 1], qt[g0 + PEER_NKEYS:g0 + 2 * PEER_NKEYS])
        rank_b, wb, cnt, wa = _route_head(a, b)
        rb_ref[0, h] = rank_b.astype(BF16)
        wb_ref[0, h] = wb.astype(BF16)
        cnt_ref[0, h] = cnt
        wa_ref[0, h] = wa


def _route(x1, sh2, sc2, g, wqt, sk, tm):
    bsz, s, d = x1.shape
    vec = pl.BlockSpec((1, 1, d), lambda b, i: (b, 0, 0))
    gate = pl.BlockSpec((1, PEER_HEADS, PEER_NKEYS, tm), lambda b, i: (b, 0, 0, i))
    gshape = lambda dt: jax.ShapeDtypeStruct((bsz, PEER_HEADS, PEER_NKEYS, s), dt)
    return pl.pallas_call(
        _route_kernel,
        grid=(bsz, s // tm),
        in_specs=[pl.BlockSpec((1, tm, d), lambda b, i: (b, i, 0)), vec, vec, _resident((1, d)),
                  _resident(wqt.shape), _resident(sk.shape)],
        out_specs=[pl.BlockSpec((1, d, tm), lambda b, i: (b, 0, i)), gate, gate, gate, gate],
        out_shape=[jax.ShapeDtypeStruct((bsz, d, s), BF16), gshape(BF16), gshape(BF16), gshape(F32), gshape(F32)],
        compiler_params=_cparams("arbitrary", "arbitrary"),
        name="route",
    )(x1, sh2, sc2, g, wqt, sk)


DENSE_ROWS = PEER_NKEYS


def _dense_kernel(h2t_ref, u_ref, vt_ref, rb_ref, wb_ref, cnt_ref, wa_ref, x_ref, gt_ref, fg_ref, fsh_ref, fsc_ref,
                  o_ref, acc_ref, p_ref, *, eblk):
    e = pl.program_id(2)

    @pl.when(e == 0)
    def _():
        acc_ref[...] = jnp.zeros_like(acc_ref)

    half = h2t_ref.shape[2] // 2
    nchunk = eblk // DENSE_ROWS
    per = DENSE_ROWS // PEER_NKEYS
    dq = acc_ref.shape[0] // nchunk

    def front(hf, c):
        cols = slice(hf * half, (hf + 1) * half)
        rows = slice(c * DENSE_ROWS, (c + 1) * DENSE_ROWS)
        gates = []
        for ii in range(c * per, (c + 1) * per):
            w = None
            for h in range(PEER_HEADS):
                cnt_i = cnt_ref[0, h, 0, ii:ii + 1, cols].astype(BF16)
                wa_i = wa_ref[0, h, 0, ii:ii + 1, cols].astype(BF16)
                wh = jnp.where(rb_ref[0, h, :, cols] < cnt_i, wb_ref[0, h, :, cols], jnp.zeros((), BF16)) * wa_i
                w = wh if w is None else w + wh
            gates.append(w)
        gate = jnp.concatenate(gates, axis=0)
        at = _dot(u_ref[rows, :], h2t_ref[0, :, cols])
        act = at * (1.0 + lax.erf(at * math.sqrt(0.5)))
        p_ref[rows, cols] = act.astype(BF16) * gate

    def back(hf, c):
        cols = slice(hf * half, (hf + 1) * half)
        out = slice(c * dq, (c + 1) * dq)
        acc_ref[out, cols] += _dot(vt_ref[0, out, :], p_ref[:, cols])

    for c in range(nchunk):
        front(0, c)
    for c in range(nchunk):
        front(1, c)
        back(0, c)
    for c in range(nchunk):
        back(1, c)

    @pl.when(e == pl.num_programs(2) - 1)
    def _():
        y = x_ref[0] + gt_ref[0] * acc_ref[...].T
        o_ref[0] = _rms(y, fg_ref[...]) * (1.0 + fsc_ref[0]) + fsh_ref[0]


def _dense(h2t, u, vt, rb, wb, cnt, wa, x1, gt2, fg, fsh, fsc, tm, eblk):
    bsz, s, d = x1.shape
    nblk = u.shape[0] // eblk
    gate = pl.BlockSpec((1, PEER_HEADS, PEER_NKEYS, tm), lambda b, i, e: (b, 0, 0, i))
    nrow = eblk // PEER_NKEYS
    rows = pl.BlockSpec((1, PEER_HEADS, 1, nrow, tm), lambda b, i, e: (b, 0, e, 0, i))
    by_block = lambda a: a.reshape(bsz, PEER_HEADS, nblk, nrow, s)
    tile = pl.BlockSpec((1, tm, d), lambda b, i, e: (b, i, 0))
    vec = pl.BlockSpec((1, 1, d), lambda b, i, e: (b, 0, 0))
    return pl.pallas_call(
        functools.partial(_dense_kernel, eblk=eblk),
        grid=(bsz, s // tm, nblk),
        in_specs=[pl.BlockSpec((1, d, tm), lambda b, i, e: (b, 0, i)),
                  pl.BlockSpec((eblk, d), lambda b, i, e: (e, 0)),
                  pl.BlockSpec((1, d, eblk), lambda b, i, e: (e, 0, 0)),
                  gate, gate, rows, rows, tile, vec, _resident((1, d)), vec, vec],
        out_specs=tile,
        out_shape=jax.ShapeDtypeStruct((bsz, s, d), F32),
        scratch_shapes=[pltpu.VMEM((d, tm), F32), pltpu.VMEM((eblk, tm), BF16)],
        compiler_params=_cparams("arbitrary", "arbitrary", "arbitrary"),
        name="dense",
    )(h2t, u, vt, rb, wb, by_block(cnt), by_block(wa), x1, gt2, fg.reshape(1, d), fsh, fsc)


TOKEN_TILE = 512
GLA_BLOCK = 1024
ATTN_Q_TILE = 2048
ATTN_K_TILE = 256
ROUTE_TILE = 256
DENSE_EXPERTS = 8 * PEER_NKEYS


def _tile(s, want):
    return want if s % want == 0 else s


def kernel(x, c, positions, w_ada, b_ada, norm1_g, w_in, gla_wa_fw, gla_ba_fw, gla_wa_bw, gla_ba_bw, gla_norm_g,
           diff_lq1, diff_lk1, diff_lq2, diff_lk2, diff_norm_g, w_gla_proj, w_diff_proj, w_out, norm2_g, peer_wq,
           peer_subkeys, peer_u, peer_v, w_final_ada, b_final_ada, normf_g):
    bsz, s, d = x.shape
    depth = w_ada.shape[0]
    tm = _tile(s, TOKEN_TILE)

    half = DIFF_HD // 2
    inv = (ROPE_THETA ** (-np.arange(0, DIFF_HD, 2, dtype=np.float32) / DIFF_HD)).astype(np.float32)
    inv_row = jnp.asarray(np.tile(inv, LANES // half).reshape(1, LANES))
    inv_col = jnp.asarray(inv.reshape(half, 1))

    fmod = _ada(c, w_final_ada, b_final_ada)
    f_shift = fmod[:, :d].reshape(bsz, 1, d)
    f_scale = fmod[:, d:].reshape(bsz, 1, d)

    qk, gv_w, lr_w = 2 * GLA_HEADS * GLA_DK, GLA_HEADS * GLA_DV, 2 * GLA_LOWRANK
    o_gq, o_gk, o_gv, o_gr = 0, qk // 2, qk, qk + gv_w
    o_lr = o_gr + gv_w
    o_dq = o_lr + lr_w
    o_dk, o_dv, o_ga, o_gb = o_dq + d, o_dq + 2 * d, o_dq + 3 * d, o_dq + 4 * d

    for l in range(depth):
        lambda_init = 0.8 - 0.6 * math.exp(-0.3 * l)
        mod = _ada(c, w_ada[l], b_ada[l]).reshape(bsz, 1, N_ADA * d)
        sh1, sc1, gt1, sh2, sc2, gt2 = (mod[:, :, i * d:(i + 1) * d] for i in range(N_ADA))

        w = w_in[l]
        col = lambda c0, n: w[:, c0:c0 + n]
        wn = jnp.concatenate([col(o_gq, qk // 2), col(o_gk, qk // 2), col(o_gv, gv_w), col(o_gr, gv_w),
                              col(o_dk, d), col(o_ga, d), col(o_gb, d)], axis=1).astype(BF16)
        wlr = jnp.zeros((d, LANES), F32).at[:, :lr_w].set(col(o_lr, lr_w)).astype(BF16)
        wt = jnp.concatenate([col(o_dq, d), col(o_dv, d), col(o_gv, gv_w)], axis=1).T.astype(BF16)
        (gq, gk, gv, gr, lr, dk1, dk2, ga, gb, dqt, dvt, gvt) = _inproj(
            x, sh1, sc1, norm1_g[l].reshape(1, d), positions, wn, wlr, wt, inv_row, inv_col, tm)

        o_f, o_b = _gla(gq, gk, gv, gvt, lr, gla_wa_fw[l], gla_ba_fw[l], gla_wa_bw[l], gla_ba_bw[l], _tile(s, GLA_BLOCK))
        o_d = _attn(dqt, dk1, dk2, dvt, diff_lq1[l], diff_lk1[l], diff_lq2[l], diff_lk2[l], lambda_init,
                    _tile(s, ATTN_Q_TILE), _tile(s, ATTN_K_TILE))
        x1 = _merge(o_f, o_b, gr, o_d, ga, gb, x, gt1, gla_norm_g[l], diff_norm_g[l],
                    w_gla_proj[l].astype(BF16), w_diff_proj[l].astype(BF16), w_out[l].astype(BF16), lambda_init, tm)

        sk = peer_subkeys[l].reshape(2 * PEER_HEADS, PEER_NKEYS, -1).astype(BF16)
        h2t, rb, wb, cnt, wa = _route(x1, sh2, sc2, norm2_g[l].reshape(1, d), peer_wq[l].T.astype(BF16), sk,
                                      _tile(s, ROUTE_TILE))
        last = l == depth - 1
        assert last, "multi-layer stacks need the final norm split out of the expert kernel"
        eblk = DENSE_EXPERTS
        vt = peer_v[l].astype(BF16).reshape(-1, eblk, d).transpose(0, 2, 1)
        x = _dense(h2t, peer_u[l].astype(BF16), vt, rb, wb, cnt, wa, x1, gt2,
                   normf_g, f_shift, f_scale, tm, eblk)
    return x
```

```python
import functools
import math

import numpy as np
import jax
import jax.numpy as jnp
from jax import lax
from jax.experimental import pallas as pl
from jax.experimental.pallas import tpu as pltpu

F32 = jnp.float32
BF16 = jnp.bfloat16

RMS_EPS = 1e-6
ROPE_THETA = 10000.0
GLA_HEADS = 4
GLA_DK = 128
GLA_DV = 256
GLA_LOWRANK = 16
GLA_GATE_NORM = 16.0
GLA_CHUNK = 64
GLA_SUB = 256
DIFF_HEADS = 8
DIFF_HD = 64
DIFF_VD = 128
PEER_HEADS = 8
PEER_NKEYS = 128
PEER_TOPK = 16
N_ADA = 6

LANES = 128
VMEM_LIMIT = 48 * 1024 * 1024

NT_DIMS = (((1,), (1,)), ((), ()))


def _cparams(*sem):
    return pltpu.CompilerParams(dimension_semantics=sem, vmem_limit_bytes=VMEM_LIMIT)


def _dot(a, b):
    return jnp.dot(a, b, preferred_element_type=F32)


def _dot_nt(a, b):
    return lax.dot_general(a, b, NT_DIMS, preferred_element_type=F32)


def _rms(x, g):
    ms = jnp.mean(x * x, axis=-1, keepdims=True)
    return x * lax.rsqrt(ms + RMS_EPS) * g


def _resident(shape):
    nd = len(shape)
    return pl.BlockSpec(shape, lambda *_: (0,) * nd)


def _ada_kernel(c_ref, w_ref, b_ref, o_ref):
    c = c_ref[...]
    ca = c * jax.nn.sigmoid(c)
    o_ref[...] = jnp.dot(ca, w_ref[...], preferred_element_type=F32,
                         precision=lax.Precision.HIGHEST) + b_ref[...]


def _ada(c, w, b):
    bsz, d = c.shape
    n = w.shape[1]
    tn = 2048
    return pl.pallas_call(
        _ada_kernel,
        grid=(n // tn,),
        in_specs=[pl.BlockSpec((bsz, d), lambda j: (0, 0)),
                  pl.BlockSpec((d, tn), lambda j: (0, j)),
                  pl.BlockSpec((1, tn), lambda j: (0, j))],
        out_specs=pl.BlockSpec((bsz, tn), lambda j: (0, j)),
        out_shape=jax.ShapeDtypeStruct((bsz, n), F32),
        compiler_params=_cparams("arbitrary"),
        name="ada",
    )(c, w, b.reshape(1, n))


_C_GQ, _C_GK, _C_GV, _C_GR, _C_DK, _C_GA, _C_GB, _C_END = 0, 512, 1024, 2048, 3072, 4096, 5120, 6144
_R_DQ, _R_DV, _R_END = 0, 1024, 2048


def _inproj_kernel(x_ref, sh_ref, sc_ref, g_ref, pos_ref, post_ref, invr_ref, invc_ref,
                   wn_ref, wlr_ref, wt_ref,
                   gq_ref, gk_ref, gv_ref, gr_ref, lr_ref, dk1_ref, dk2_ref, ga_ref, gb_ref,
                   dqt_ref, dvt_ref):
    tm = x_ref.shape[1]
    h = _rms(x_ref[0], g_ref[...]) * (1.0 + sc_ref[0]) + sh_ref[0]
    hb = h.astype(BF16)

    def proj(c0, c1):
        return _dot(hb, wn_ref[:, c0:c1])

    gq_ref[0] = (proj(_C_GQ, _C_GK) * (GLA_DK ** -0.5)).astype(BF16)
    gk_ref[0] = proj(_C_GK, _C_GV).astype(BF16)
    gv_ref[0] = proj(_C_GV, _C_GR).astype(BF16)
    gr_ref[0] = proj(_C_GR, _C_DK).astype(BF16)
    ga_ref[0] = proj(_C_GA, _C_GB).astype(BF16)
    gb_ref[0] = proj(_C_GB, _C_END).astype(BF16)
    lr_ref[0] = _dot(hb, wlr_ref[...])

    ang = pos_ref[0] * invr_ref[...]
    cs = jnp.cos(ang)
    sn = jnp.sin(ang)
    lane = lax.broadcasted_iota(jnp.int32, (tm, LANES), 1)
    first = (lane % DIFF_HD) < (DIFF_HD // 2)
    sn = jnp.where(first, -sn, sn)
    y = proj(_C_DK, _C_GA)
    pad = jnp.where(lane == DIFF_HD, 1.0, 0.0)
    for hd in range(DIFF_HEADS):
        yb = y[:, hd * LANES:(hd + 1) * LANES]
        rot = jnp.where(first, pltpu.roll(yb, LANES - DIFF_HD // 2, 1), pltpu.roll(yb, DIFF_HD // 2, 1))
        kr = yb * cs + rot * sn
        dk1_ref[0, :, hd * LANES:(hd + 1) * LANES] = jnp.where(lane < DIFF_HD, kr, pad).astype(BF16)
        dk2_ref[0, :, hd * LANES:(hd + 1) * LANES] = jnp.where(lane < DIFF_HD, pltpu.roll(kr, DIFF_HD, 1),
                                                                pad).astype(BF16)

    angt = invc_ref[...] * post_ref[0]
    ct = jnp.cos(angt)
    st = jnp.sin(angt)
    half = DIFF_HD // 2
    yt = _dot_nt(wt_ref[_R_DQ:_R_DV, :], hb)
    qscale = DIFF_HD ** -0.5 * math.log2(math.e)
    for blk in range(2 * DIFF_HEADS):
        r0 = blk * DIFF_HD
        x1 = yt[r0:r0 + half]
        x2 = yt[r0 + half:r0 + DIFF_HD]
        dqt_ref[0, r0:r0 + half, :] = ((x1 * ct - x2 * st) * qscale).astype(BF16)
        dqt_ref[0, r0 + half:r0 + DIFF_HD, :] = ((x2 * ct + x1 * st) * qscale).astype(BF16)
    dvt_ref[0] = _dot_nt(wt_ref[_R_DV:_R_END, :], hb).astype(BF16)


def _inproj(x, sh, sc, g, pos, wn, wlr, wt, inv_row, inv_col, tm):
    bsz, s, d = x.shape
    posf = pos.astype(F32)
    nat = lambda n: pl.BlockSpec((1, tm, n), lambda b, i: (b, i, 0))
    tr = lambda n: pl.BlockSpec((1, n, tm), lambda b, i: (b, 0, i))
    vec = pl.BlockSpec((1, 1, d), lambda b, i: (b, 0, 0))
    shp = lambda *dims: jax.ShapeDtypeStruct(dims, BF16)
    return pl.pallas_call(
        _inproj_kernel,
        grid=(bsz, s // tm),
        in_specs=[nat(d), vec, vec, _resident((1, d)),
                  pl.BlockSpec((1, tm, 1), lambda b, i: (b, i, 0)),
                  pl.BlockSpec((1, 1, tm), lambda b, i: (b, 0, i)),
                  _resident(inv_row.shape), _resident(inv_col.shape),
                  _resident(wn.shape), _resident(wlr.shape), _resident(wt.shape)],
        out_specs=[nat(512), nat(512), nat(1024), nat(1024), nat(LANES), nat(1024), nat(1024), nat(1024), nat(1024),
                   tr(1024), tr(1024)],
        out_shape=[shp(bsz, s, 512), shp(bsz, s, 512), shp(bsz, s, 1024), shp(bsz, s, 1024),
                   jax.ShapeDtypeStruct((bsz, s, LANES), F32),
                   shp(bsz, s, 1024), shp(bsz, s, 1024), shp(bsz, s, 1024), shp(bsz, s, 1024),
                   shp(bsz, 1024, s), shp(bsz, 1024, s)],
        compiler_params=_cparams("arbitrary", "arbitrary"),
        name="inproj",
    )(x, sh, sc, g, posf.reshape(bsz, s, 1), posf.reshape(bsz, 1, s), inv_row, inv_col, wn, wlr, wt)


def _gla_direction(q_ref, k_ref, v_ref, lr_ref, wh_ref, wl_ref, ba_ref, tri_ref, o_ref, s_ref, reverse):
    blk = q_ref.shape[1]
    lr = lr_ref[0]
    lr_hi = lr.astype(BF16)
    lr_lo = (lr - lr_hi.astype(F32)).astype(BF16)
    z = _dot(lr_hi, wh_ref[...]) + _dot(lr_lo, wh_ref[...]) + _dot(lr_hi, wl_ref[...]) + ba_ref[...]
    logg = -(jnp.maximum(-z, 0.0) + jnp.log1p(jnp.exp(-jnp.abs(z)))) * (1.0 / GLA_GATE_NORM)
    lg_hi = logg.astype(BF16)
    lg_lo = (logg - lg_hi.astype(F32)).astype(BF16)
    yield
    tri = tri_ref[...]
    sub = tri.shape[0]
    q = q_ref[0].astype(F32)
    k = k_ref[0].astype(F32)
    v = v_ref[0]
    b, qe, o_intra = [], [], []
    for r0 in range(0, blk, sub):
        rows = slice(r0, r0 + sub)
        b_s = _dot(tri, lg_hi[rows]) + _dot(tri, lg_lo[rows])
        qe_s = (q[rows] * jnp.exp(b_s)).astype(BF16)
        ke_s = (k[rows] * jnp.exp(-b_s)).astype(BF16)
        att = jnp.where(tri > 0, _dot_nt(qe_s, ke_s), 0.0).astype(BF16)
        b.append(b_s)
        qe.append(qe_s)
        o_intra.append(_dot(att, v[rows]))
        yield
    b = jnp.concatenate(b, axis=0)
    qe = jnp.concatenate(qe, axis=0)
    o_intra = jnp.concatenate(o_intra, axis=0)

    state = s_ref[...]
    nchunk = blk // GLA_CHUNK
    order = range(nchunk - 1, -1, -1) if reverse else range(nchunk)
    for n in order:
        r0 = n * GLA_CHUNK
        edge = r0 if reverse else r0 + GLA_CHUNK - 1
        b_c = b[r0:r0 + GLA_CHUNK]
        b_e = b[edge:edge + 1]
        kend = (k[r0:r0 + GLA_CHUNK] * jnp.exp(b_e - b_c)).astype(BF16)
        o_inter = _dot_nt(qe[r0:r0 + GLA_CHUNK], state.astype(BF16))
        o_ref[0, r0:r0 + GLA_CHUNK, :] = (o_intra[r0:r0 + GLA_CHUNK] + o_inter).astype(BF16)
        state = state * jnp.exp(b_e) + lax.dot_general(v[r0:r0 + GLA_CHUNK], kend, (((0,), (0,)), ((), ())),
                                                         preferred_element_type=F32)
        yield
    s_ref[...] = state


def _gla_kernel(qf, kf, vf, lrf, qb, kb, vb, lrb,
                wfh, wfl, baf, wbh, wbl, bab, trif, trib,
                of_ref, ob_ref, sf_ref, sb_ref):
    @pl.when(pl.program_id(2) == 0)
    def _():
        sf_ref[...] = jnp.zeros_like(sf_ref)
        sb_ref[...] = jnp.zeros_like(sb_ref)

    live = [_gla_direction(qf, kf, vf, lrf, wfh, wfl, baf, trif, of_ref, sf_ref, reverse=False),
            _gla_direction(qb, kb, vb, lrb, wbh, wbl, bab, trib, ob_ref, sb_ref, reverse=True)]
    while live:
        live = [d for d in live if next(d, True) is None]


def _gla(gq, gk, gv, lr, wa_f, ba_f, wa_b, ba_b, blk):
    bsz, s, _ = gq.shape
    nblk = s // blk
    def pad_rows(w, r0):
        return jnp.zeros((LANES, w.shape[1]), F32).at[r0:r0 + GLA_LOWRANK].set(w)

    def hi_lo(w):
        hi = w.astype(BF16)
        return hi, (w - hi.astype(F32)).astype(BF16)

    wfh, wfl = hi_lo(pad_rows(wa_f, 0))
    wbh, wbl = hi_lo(pad_rows(wa_b, GLA_LOWRANK))
    sub = min(blk, GLA_SUB)
    r = np.arange(sub)
    same = (r[:, None] // GLA_CHUNK) == (r[None, :] // GLA_CHUNK)
    tri_f = jnp.asarray(same & (r[None, :] <= r[:, None]), BF16)
    tri_b = jnp.asarray(same & (r[None, :] >= r[:, None]), BF16)

    fw = lambda b, h, i: (b, i, h)
    bw = lambda b, h, i: (b, nblk - 1 - i, h)

    def seq_specs(tok):
        return [pl.BlockSpec((1, blk, GLA_DK), tok), pl.BlockSpec((1, blk, GLA_DK), tok),
                pl.BlockSpec((1, blk, GLA_DV), tok),
                pl.BlockSpec((1, blk, LANES), lambda b, h, i, t=tok: (t(b, h, i)[0], t(b, h, i)[1], 0))]

    wspec = pl.BlockSpec((LANES, GLA_DK), lambda b, h, i: (0, h))
    bspec = pl.BlockSpec((1, GLA_DK), lambda b, h, i: (0, h))
    out_f = pl.BlockSpec((1, blk, GLA_DV), fw)
    out_b = pl.BlockSpec((1, blk, GLA_DV), bw)
    o_shape = jax.ShapeDtypeStruct((bsz, s, GLA_HEADS * GLA_DV), BF16)
    return pl.pallas_call(
        _gla_kernel,
        grid=(bsz, GLA_HEADS, nblk),
        in_specs=seq_specs(fw) + seq_specs(bw) + [wspec, wspec, bspec, wspec, wspec, bspec,
                                                             _resident((sub, sub)), _resident((sub, sub))],
        out_specs=[out_f, out_b],
        out_shape=[o_shape, o_shape],
        scratch_shapes=[pltpu.VMEM((GLA_DV, GLA_DK), F32), pltpu.VMEM((GLA_DV, GLA_DK), F32)],
        compiler_params=_cparams("arbitrary", "arbitrary", "arbitrary"),
        name="gla",
    )(gq, gk, gv, lr, gq, gk, gv, lr,
      wfh, wfl, ba_f.reshape(1, -1), wbh, wbl, ba_b.reshape(1, -1), tri_f, tri_b)


ATTN_SAFE_SHIFT = 40.0
ATTN_NORM_CHUNK = 1024


def _ring(nk, nslot, stages):
    assert nk % nslot == 0
    offs = [off for _, off in stages]

    def static_iter(i):
        for fn, off in stages:
            if 0 <= i + off < nk:
                fn(i + off, (i + off) % nslot)

    def full(g):
        return all(0 <= g * nslot + u + off < nk for u in range(nslot) for off in offs)

    ngroup = nk // nslot
    interior = [g for g in range(ngroup) if full(g)]
    g_lo, g_hi = (interior[0], interior[-1] + 1) if interior else (ngroup, ngroup)
    for i in range(-max(offs), g_lo * nslot):
        static_iter(i)
    if g_hi - g_lo == 1:
        for i in range(g_lo * nslot, g_hi * nslot):
            static_iter(i)
    elif g_hi > g_lo:
        def group(g, carry):
            for u in range(nslot):
                for fn, off in stages:
                    fn(g * nslot + u + off, (u + off) % nslot)
            return carry
        lax.fori_loop(g_lo, g_hi, group, 0)
    for i in range(g_hi * nslot, nk - min(offs)):
        static_iter(i)


def _attn_kernel(lq1_ref, lk1_ref, lq2_ref, lk2_ref, qt_ref, k1_ref, k2_ref, vt_ref, o_ref,
                 s_ref, p_ref, a_ref, mt_ref, m_ref, l_ref, acc_ref, kmax_ref, *, tk, lambda_init):
    k_refs = (k1_ref, k2_ref)
    nkeys = k1_ref.shape[1]
    nk = nkeys // tk
    nslot = p_ref.shape[0]
    dist = nslot // 2
    nslot_slow = s_ref.shape[0]
    dist_slow = nslot_slow // 2
    tq = qt_ref.shape[2]
    nchunk = max(nkeys // ATTN_NORM_CHUNK, 1)
    kc = nkeys // nchunk

    @pl.when(pl.program_id(2) == 0)
    def _():
        lane = lax.broadcasted_iota(jnp.int32, (kc, LANES), 1)
        for mi in range(2):
            def chunk(i, mx):
                kk = k_refs[mi][0, pl.ds(pl.multiple_of(i * kc, kc), kc), :].astype(F32)
                kk = jnp.where(lane < DIFF_HD, kk, 0.0)
                n2 = jnp.sum(kk * kk, axis=1, keepdims=True)
                return jnp.maximum(mx, jnp.max(n2, axis=0, keepdims=True))
            kmax_ref[mi] = jnp.sqrt(lax.fori_loop(0, nchunk, chunk, jnp.zeros((1, 1), F32)))

    qt = qt_ref[0]
    row = lax.broadcasted_iota(jnp.int32, (16, tq), 0)
    q_shift, q_plain, bound = [], [], []
    for mi in range(2):
        qm = qt[mi * DIFF_HD:(mi + 1) * DIFF_HD]
        qf = qm.astype(F32)
        c = jnp.sqrt(jnp.sum(qf * qf, axis=0, keepdims=True)) * kmax_ref[mi]
        c = c.astype(BF16).astype(F32)
        shift_rows = jnp.where(row == 0, -c, 0.0).astype(BF16)
        q_shift.append(jnp.concatenate([qm, shift_rows, jnp.zeros((DIFF_HD - 16, tq), BF16)], axis=0))
        q_plain.append(jnp.concatenate([qm, jnp.zeros((DIFF_HD, tq), BF16)], axis=0))
        bound.append(jnp.max(c))
    safe = jnp.maximum(bound[0], bound[1]) <= ATTN_SAFE_SHIFT
    acc_ref[...] = jnp.zeros_like(acc_ref)
    l_ref[...] = jnp.zeros_like(l_ref)

    def key_tile(mi, t):
        return k_refs[mi][0, pl.ds(pl.multiple_of(t * tk, tk), tk), :]

    def value_tile(t):
        return vt_ref[0, :, pl.ds(pl.multiple_of(t * tk, tk), tk)]

    @pl.when(safe)
    def _():
        def probs(t, slot):
            for mi in range(2):
                p = jnp.exp2(_dot(key_tile(mi, t), q_shift[mi]))
                l_ref[mi] += jnp.sum(p, axis=0, keepdims=True)
                p_ref[slot, mi] = p.astype(BF16)

        def values(t, slot):
            vtj = value_tile(t)
            for mi in range(2):
                acc_ref[mi] += _dot(vtj, p_ref[slot, mi])

        _ring(nk, nslot, [(values, 0), (probs, dist)])

    @pl.when(jnp.logical_not(safe))
    def _():
        m_ref[...] = jnp.full(m_ref.shape, -jnp.inf, F32)

        def scores(t, slot):
            for mi in range(2):
                s = _dot(key_tile(mi, t), q_plain[mi])
                s_ref[slot, mi] = s
                mt_ref[slot, mi] = jnp.max(s, axis=0, keepdims=True)

        def softmax(t, slot):
            for mi in range(2):
                m_old = m_ref[mi]
                m_new = jnp.maximum(m_old, mt_ref[slot, mi])
                a_ref[slot, mi] = jnp.exp2(m_old - m_new)
                p = jnp.exp2(s_ref[slot, mi] - m_new)
                l_ref[mi] = a_ref[slot, mi] * l_ref[mi] + jnp.sum(p, axis=0, keepdims=True)
                p_ref[slot, mi] = p.astype(BF16)
                m_ref[mi] = m_new

        def values(t, slot):
            vtj = value_tile(t)
            for mi in range(2):
                acc_ref[mi] = acc_ref[mi] * a_ref[slot, mi] + _dot(vtj, p_ref[slot, mi])

        _ring(nk, nslot_slow, [(softmax, 0), (values, -dist_slow), (scores, dist_slow)])

    lam = (jnp.exp(jnp.sum(lq1_ref[...] * lk1_ref[...], axis=-1, keepdims=True))
           - jnp.exp(jnp.sum(lq2_ref[...] * lk2_ref[...], axis=-1, keepdims=True)) + lambda_init)
    ot = acc_ref[0] * (1.0 / l_ref[0]) - lam * (acc_ref[1] * (1.0 / l_ref[1]))
    o_ref[0] = ot.T.astype(BF16)


def _attn(dqt, dk1, dk2, dvt, lq1, lk1, lq2, lk2, lambda_init, tq, tk):
    bsz, s, _ = dk1.shape
    lspec = _resident((1, DIFF_HD))
    kspec = pl.BlockSpec((1, s, LANES), lambda b, h, i: (b, 0, h))
    return pl.pallas_call(
        functools.partial(_attn_kernel, tk=tk, lambda_init=lambda_init),
        grid=(bsz, DIFF_HEADS, s // tq),
        in_specs=[lspec, lspec, lspec, lspec,
                  pl.BlockSpec((1, 2 * DIFF_HD, tq), lambda b, h, i: (b, h, i)),
                  kspec, kspec,
                  pl.BlockSpec((1, DIFF_VD, s), lambda b, h, i: (b, h, 0))],
        out_specs=pl.BlockSpec((1, tq, DIFF_VD), lambda b, h, i: (b, i, h)),
        out_shape=jax.ShapeDtypeStruct((bsz, s, DIFF_HEADS * DIFF_VD), BF16),
        scratch_shapes=[pltpu.VMEM((2, 2, tk, tq), F32), pltpu.VMEM((4, 2, tk, tq), BF16),
                        pltpu.VMEM((4, 2, 1, tq), F32), pltpu.VMEM((4, 2, 1, tq), F32),
                        pltpu.VMEM((2, 1, tq), F32), pltpu.VMEM((2, 1, tq), F32),
                        pltpu.VMEM((2, DIFF_VD, tq), F32),
                        pltpu.VMEM((2, 1, 1), F32)],
        compiler_params=_cparams("arbitrary", "arbitrary", "arbitrary"),
        name="attn",
    )(lq1.reshape(1, -1), lk1.reshape(1, -1), lq2.reshape(1, -1), lk2.reshape(1, -1), dqt, dk1, dk2, dvt)


def _merge_kernel(of_ref, ob_ref, gr_ref, od_ref, ga_ref, gb_ref, x_ref, gt_ref, gg_ref, dg_ref,
                  wg_ref, wd_ref, wo_ref, o_ref, *, lambda_init):
    og = of_ref[0].astype(F32) + ob_ref[0].astype(F32)
    gg = gg_ref[...]
    og = jnp.concatenate(
        [_rms(og[:, h * GLA_DV:(h + 1) * GLA_DV], gg[:, h * GLA_DV:(h + 1) * GLA_DV]) for h in range(GLA_HEADS)],
        axis=-1)
    gr = gr_ref[0].astype(F32)
    y_gla = _dot((og * (gr * jax.nn.sigmoid(gr))).astype(BF16), wg_ref[...])
    od = od_ref[0].astype(F32)
    dg = dg_ref[...]
    od = jnp.concatenate(
        [_rms(od[:, h * DIFF_VD:(h + 1) * DIFF_VD], dg[:, h * DIFF_VD:(h + 1) * DIFF_VD]) for h in range(DIFF_HEADS)],
        axis=-1) * (1.0 - lambda_init)
    y_diff = _dot(od.astype(BF16), wd_ref[...])
    merged = (jax.nn.sigmoid(ga_ref[0].astype(F32)) * y_gla + jax.nn.sigmoid(gb_ref[0].astype(F32)) * y_diff)
    o_ref[0] = x_ref[0] + gt_ref[0] * _dot(merged.astype(BF16), wo_ref[...])


def _merge(o_f, o_b, gr, o_d, ga, gb, x, gt1, gla_g, diff_g, wg, wd, wo, lambda_init, tm):
    bsz, s, d = x.shape
    tile = pl.BlockSpec((1, tm, d), lambda b, i: (b, i, 0))
    vec = pl.BlockSpec((1, 1, d), lambda b, i: (b, 0, 0))
    return pl.pallas_call(
        functools.partial(_merge_kernel, lambda_init=lambda_init),
        grid=(bsz, s // tm),
        in_specs=[tile] * 7 + [vec, _resident((1, d)), _resident((1, d)),
                               _resident((d, d)), _resident((d, d)), _resident((d, d))],
        out_specs=tile,
        out_shape=jax.ShapeDtypeStruct((bsz, s, d), F32),
        compiler_params=_cparams("arbitrary", "arbitrary"),
        name="merge",
    )(o_f, o_b, gr, o_d, ga, gb, x, gt1, gla_g.reshape(1, d), diff_g.reshape(1, d), wg, wd, wo)


_CAND_ROW_COUNT = (16, 8, 5, 4, 3, 2, 2, 2)
_SUB = 8
_SENTINEL = 2.0 ** 100


def _top16(a):
    work = a
    vals = []
    for r in range(PEER_TOPK):
        m = jnp.max(work, axis=0, keepdims=True)
        vals.append(m)
        work = jnp.where(work == m, -_SENTINEL * (1.0 + r / 32.0), work)
    rank = jnp.where(work <= -_SENTINEL, (work * (-1.0 / _SENTINEL) - 1.0) * 32.0, float(PEER_TOPK))
    return jnp.concatenate(vals, axis=0), rank


def _oddeven_merge_sort_pairs(n):
    pairs = []
    p = 1
    while p < n:
        k = p
        while k >= 1:
            for j in range(k % p, n - k, 2 * k):
                for i in range(min(k, n - j - k)):
                    if (i + j) // (2 * p) == (i + j + k) // (2 * p):
                        pairs.append((i + j, i + j + k))
            k //= 2
        p *= 2
    return pairs


def _top16_values(a):
    nslab = a.shape[0] // _SUB
    assert nslab == PEER_TOPK
    slabs = [a[k * _SUB:(k + 1) * _SUB] for k in range(nslab)]

    def order(i, j):
        slabs[i], slabs[j] = jnp.maximum(slabs[i], slabs[j]), jnp.minimum(slabs[i], slabs[j])

    for i, j in _oddeven_merge_sort_pairs(nslab):
        order(i, j)
    shift = _SUB // 2
    while shift >= 1:
        other = [pltpu.roll(x, shift, 0) for x in slabs]
        slabs = [jnp.maximum(slabs[k], other[nslab - 1 - k]) for k in range(nslab)]
        d = nslab // 2
        while d >= 1:
            for k in range(nslab):
                if k & d == 0:
                    order(k, k + d)
            d //= 2
        shift //= 2
    return jnp.concatenate([x[0:1] for x in slabs], axis=0)


def _route_head(a, b):
    t = a.shape[1]
    top_a = _top16_values(a)
    top_b, rank_b = _top16(b)
    rowi = lax.broadcasted_iota(jnp.int32, (_SUB, t), 0)
    pieces = [top_b + top_a[0:1]]
    for r in range(1, _SUB):
        piece = top_b[0:_SUB] + top_a[r:r + 1]
        n = _CAND_ROW_COUNT[r]
        pieces.append(piece if n == _SUB else jnp.where(rowi < n, piece, -jnp.inf))
    pieces.append(top_a[_SUB:PEER_TOPK] + top_b[0:1])
    cand = jnp.concatenate(pieces, axis=0)
    work = cand
    thr = None
    for _ in range(PEER_TOPK):
        thr = jnp.max(work, axis=0, keepdims=True)
        work = jnp.where(work == thr, -jnp.inf, work)
    sel = cand >= thr
    smax = top_a[0:1] + top_b[0:1]
    z = jnp.sum(jnp.where(sel, jnp.exp(cand - smax), 0.0), axis=0, keepdims=True)
    self32 = sel.astype(F32)
    counts = [jnp.sum(self32[0:PEER_TOPK], axis=0, keepdims=True)]
    for r in range(1, _SUB):
        r0 = PEER_TOPK + (r - 1) * _SUB
        counts.append(jnp.sum(self32[r0:r0 + _SUB], axis=0, keepdims=True))
    tail0 = PEER_TOPK + (_SUB - 1) * _SUB
    for r in range(_SUB, PEER_TOPK):
        counts.append(self32[tail0 + r - _SUB:tail0 + r - _SUB + 1])
    cnt = jnp.zeros(a.shape, F32)
    for r in range(PEER_TOPK):
        cnt = jnp.where(a == top_a[r:r + 1], counts[r], cnt)
    wa = jnp.exp(a - top_a[0:1]) * (0.5 / z)
    wb = jnp.exp(b - top_b[0:1])
    return rank_b, wb, cnt, wa


def _route_kernel(x_ref, sh_ref, sc_ref, g_ref, wqt_ref, sk_ref, h2t_ref, rb_ref, wb_ref, cnt_ref, wa_ref):
    h2 = _rms(x_ref[0], g_ref[...]) * (1.0 + sc_ref[0]) + sh_ref[0]
    hb = h2.astype(BF16)
    h2t_ref[0] = h2.T.astype(BF16)
    qt = _dot_nt(wqt_ref[...], hb).astype(BF16)
    for h in range(PEER_HEADS):
        g0 = 2 * h * PEER_NKEYS
        a = _dot(sk_ref[2 * h], qt[g0:g0 + PEER_NKEYS])
        b = _dot(sk_ref[2 * h + 1], qt[g0 + PEER_NKEYS:g0 + 2 * PEER_NKEYS])
        rank_b, wb, cnt, wa = _route_head(a, b)
        rb_ref[0, h] = rank_b.astype(BF16)
        wb_ref[0, h] = wb.astype(BF16)
        cnt_ref[0, h] = cnt
        wa_ref[0, h] = wa


def _route(x1, sh2, sc2, g, wqt, sk, tm):
    bsz, s, d = x1.shape
    vec = pl.BlockSpec((1, 1, d), lambda b, i: (b, 0, 0))
    gate = pl.BlockSpec((1, PEER_HEADS, PEER_NKEYS, tm), lambda b, i: (b, 0, 0, i))
    gshape = lambda dt: jax.ShapeDtypeStruct((bsz, PEER_HEADS, PEER_NKEYS, s), dt)
    return pl.pallas_call(
        _route_kernel,
        grid=(bsz, s // tm),
        in_specs=[pl.BlockSpec((1, tm, d), lambda b, i: (b, i, 0)), vec, vec, _resident((1, d)),
                  _resident(wqt.shape), _resident(sk.shape)],
        out_specs=[pl.BlockSpec((1, d, tm), lambda b, i: (b, 0, i)), gate, gate, gate, gate],
        out_shape=[jax.ShapeDtypeStruct((bsz, d, s), BF16), gshape(BF16), gshape(BF16), gshape(F32), gshape(F32)],
        compiler_params=_cparams("arbitrary", "arbitrary"),
        name="route",
    )(x1, sh2, sc2, g, wqt, sk)


DENSE_ROWS = PEER_NKEYS


def _dense_kernel(h2t_ref, u_ref, vt_ref, rb_ref, wb_ref, cnt_ref, wa_ref, x_ref, gt_ref, fg_ref, fsh_ref, fsc_ref,
                  o_ref, acc_ref, p_ref, *, eblk):
    e = pl.program_id(2)

    @pl.when(e == 0)
    def _():
        acc_ref[...] = jnp.zeros_like(acc_ref)

    half = h2t_ref.shape[2] // 2
    nchunk = eblk // DENSE_ROWS
    per = DENSE_ROWS // PEER_NKEYS
    dq = acc_ref.shape[0] // nchunk

    def front(hf, c):
        cols = slice(hf * half, (hf + 1) * half)
        rows = slice(c * DENSE_ROWS, (c + 1) * DENSE_ROWS)
        gates = []
        for ii in range(c * per, (c + 1) * per):
            w = None
            for h in range(PEER_HEADS):
                cnt_i = cnt_ref[0, h, 0, ii:ii + 1, cols].astype(BF16)
                wa_i = wa_ref[0, h, 0, ii:ii + 1, cols].astype(BF16)
                wh = jnp.where(rb_ref[0, h, :, cols] < cnt_i, wb_ref[0, h, :, cols], jnp.zeros((), BF16)) * wa_i
                w = wh if w is None else w + wh
            gates.append(w)
        gate = jnp.concatenate(gates, axis=0)
        at = _dot(u_ref[rows, :], h2t_ref[0, :, cols])
        act = at * (1.0 + lax.erf(at * math.sqrt(0.5)))
        p_ref[rows, cols] = act.astype(BF16) * gate

    def back(hf, c):
        cols = slice(hf * half, (hf + 1) * half)
        out = slice(c * dq, (c + 1) * dq)
        acc_ref[out, cols] += _dot(vt_ref[0, out, :], p_ref[:, cols])

    for c in range(nchunk):
        front(0, c)
    for c in range(nchunk):
        front(1, c)
        back(0, c)
    for c in range(nchunk):
        back(1, c)

    @pl.when(e == pl.num_programs(2) - 1)
    def _():
        y = x_ref[0] + gt_ref[0] * acc_ref[...].T
        o_ref[0] = _rms(y, fg_ref[...]) * (1.0 + fsc_ref[0]) + fsh_ref[0]


def _dense(h2t, u, vt, rb, wb, cnt, wa, x1, gt2, fg, fsh, fsc, tm, eblk):
    bsz, s, d = x1.shape
    nblk = u.shape[0] // eblk
    gate = pl.BlockSpec((1, PEER_HEADS, PEER_NKEYS, tm), lambda b, i, e: (b, 0, 0, i))
    nrow = eblk // PEER_NKEYS
    rows = pl.BlockSpec((1, PEER_HEADS, 1, nrow, tm), lambda b, i, e: (b, 0, e, 0, i))
    by_block = lambda a: a.reshape(bsz, PEER_HEADS, nblk, nrow, s)
    tile = pl.BlockSpec((1, tm, d), lambda b, i, e: (b, i, 0))
    vec = pl.BlockSpec((1, 1, d), lambda b, i, e: (b, 0, 0))
    return pl.pallas_call(
        functools.partial(_dense_kernel, eblk=eblk),
        grid=(bsz, s // tm, nblk),
        in_specs=[pl.BlockSpec((1, d, tm), lambda b, i, e: (b, 0, i)),
                  pl.BlockSpec((eblk, d), lambda b, i, e: (e, 0)),
                  pl.BlockSpec((1, d, eblk), lambda b, i, e: (e, 0, 0)),
                  gate, gate, rows, rows, tile, vec, _resident((1, d)), vec, vec],
        out_specs=tile,
        out_shape=jax.ShapeDtypeStruct((bsz, s, d), F32),
        scratch_shapes=[pltpu.VMEM((d, tm), F32), pltpu.VMEM((eblk, tm), BF16)],
        compiler_params=_cparams("arbitrary", "arbitrary", "arbitrary"),
        name="dense",
    )(h2t, u, vt, rb, wb, by_block(cnt), by_block(wa), x1, gt2, fg.reshape(1, d), fsh, fsc)


TOKEN_TILE = 512
GLA_BLOCK = 1024
ATTN_Q_TILE = 2048
ATTN_K_TILE = 256
ROUTE_TILE = 256
DENSE_EXPERTS = 8 * PEER_NKEYS


def _tile(s, want):
    return want if s % want == 0 else s


def kernel(x, c, positions, w_ada, b_ada, norm1_g, w_in, gla_wa_fw, gla_ba_fw, gla_wa_bw, gla_ba_bw, gla_norm_g,
           diff_lq1, diff_lk1, diff_lq2, diff_lk2, diff_norm_g, w_gla_proj, w_diff_proj, w_out, norm2_g, peer_wq,
           peer_subkeys, peer_u, peer_v, w_final_ada, b_final_ada, normf_g):
    bsz, s, d = x.shape
    depth = w_ada.shape[0]
    tm = _tile(s, TOKEN_TILE)

    half = DIFF_HD // 2
    inv = (ROPE_THETA ** (-np.arange(0, DIFF_HD, 2, dtype=np.float32) / DIFF_HD)).astype(np.float32)
    inv_row = jnp.asarray(np.tile(inv, LANES // half).reshape(1, LANES))
    inv_col = jnp.asarray(inv.reshape(half, 1))

    fmod = _ada(c, w_final_ada, b_final_ada)
    f_shift = fmod[:, :d].reshape(bsz, 1, d)
    f_scale = fmod[:, d:].reshape(bsz, 1, d)

    qk, gv_w, lr_w = 2 * GLA_HEADS * GLA_DK, GLA_HEADS * GLA_DV, 2 * GLA_LOWRANK
    o_gq, o_gk, o_gv, o_gr = 0, qk // 2, qk, qk + gv_w
    o_lr = o_gr + gv_w
    o_dq = o_lr + lr_w
    o_dk, o_dv, o_ga, o_gb = o_dq + d, o_dq + 2 * d, o_dq + 3 * d, o_dq + 4 * d

    for l in range(depth):
        lambda_init = 0.8 - 0.6 * math.exp(-0.3 * l)
        mod = _ada(c, w_ada[l], b_ada[l]).reshape(bsz, 1, N_ADA * d)
        sh1, sc1, gt1, sh2, sc2, gt2 = (mod[:, :, i * d:(i + 1) * d] for i in range(N_ADA))

        w = w_in[l]
        col = lambda c0, n: w[:, c0:c0 + n]
        wn = jnp.concatenate([col(o_gq, qk // 2), col(o_gk, qk // 2), col(o_gv, gv_w), col(o_gr, gv_w),
                              col(o_dk, d), col(o_ga, d), col(o_gb, d)], axis=1).astype(BF16)
        wlr = jnp.zeros((d, LANES), F32).at[:, :lr_w].set(col(o_lr, lr_w)).astype(BF16)
        wt = jnp.concatenate([col(o_dq, d), col(o_dv, d)], axis=1).T.astype(BF16)
        (gq, gk, gv, gr, lr, dk1, dk2, ga, gb, dqt, dvt) = _inproj(
            x, sh1, sc1, norm1_g[l].reshape(1, d), positions, wn, wlr, wt, inv_row, inv_col, tm)

        o_f, o_b = _gla(gq, gk, gv, lr, gla_wa_fw[l], gla_ba_fw[l], gla_wa_bw[l], gla_ba_bw[l], _tile(s, GLA_BLOCK))
        o_d = _attn(dqt, dk1, dk2, dvt, diff_lq1[l], diff_lk1[l], diff_lq2[l], diff_lk2[l], lambda_init,
                    _tile(s, ATTN_Q_TILE), _tile(s, ATTN_K_TILE))
        x1 = _merge(o_f, o_b, gr, o_d, ga, gb, x, gt1, gla_norm_g[l], diff_norm_g[l],
                    w_gla_proj[l].astype(BF16), w_diff_proj[l].astype(BF16), w_out[l].astype(BF16), lambda_init, tm)

        sk = peer_subkeys[l].reshape(2 * PEER_HEADS, PEER_NKEYS, -1).astype(BF16)
        h2t, rb, wb, cnt, wa = _route(x1, sh2, sc2, norm2_g[l].reshape(1, d), peer_wq[l].T.astype(BF16), sk,
                                      _tile(s, ROUTE_TILE))
        last = l == depth - 1
        assert last, "multi-layer stacks need the final norm split out of the expert kernel"
        eblk = DENSE_EXPERTS
        vt = peer_v[l].astype(BF16).reshape(-1, eblk, d).transpose(0, 2, 1)
        x = _dense(h2t, peer_u[l].astype(BF16), vt, rb, wb, cnt, wa, x1, gt2,
                   normf_g, f_shift, f_scale, tm, eblk)
    return x
```

```python
import functools
import math

import numpy as np
import jax
import jax.numpy as jnp
from jax import lax
from jax.experimental import pallas as pl
from jax.experimental.pallas import tpu as pltpu

F32 = jnp.float32
BF16 = jnp.bfloat16

RMS_EPS = 1e-6
ROPE_THETA = 10000.0
GLA_HEADS = 4
GLA_DK = 128
GLA_DV = 256
GLA_LOWRANK = 16
GLA_GATE_NORM = 16.0
GLA_CHUNK = 64
GLA_SUB = 256
DIFF_HEADS = 8
DIFF_HD = 64
DIFF_VD = 128
PEER_HEADS = 8
PEER_NKEYS = 128
PEER_TOPK = 16
N_ADA = 6

LANES = 128
VMEM_LIMIT = 48 * 1024 * 1024

NT_DIMS = (((1,), (1,)), ((), ()))


def _cparams(*sem):
    return pltpu.CompilerParams(dimension_semantics=sem, vmem_limit_bytes=VMEM_LIMIT)


def _dot(a, b):
    return jnp.dot(a, b, preferred_element_type=F32)


def _dot_nt(a, b):
    return lax.dot_general(a, b, NT_DIMS, preferred_element_type=F32)


def _rms(x, g):
    ms = jnp.mean(x * x, axis=-1, keepdims=True)
    return x * lax.rsqrt(ms + RMS_EPS) * g


def _resident(shape):
    nd = len(shape)
    return pl.BlockSpec(shape, lambda *_: (0,) * nd)


def _ada_kernel(c_ref, w_ref, b_ref, o_ref):
    c = c_ref[...]
    ca = c * jax.nn.sigmoid(c)
    o_ref[...] = jnp.dot(ca, w_ref[...], preferred_element_type=F32,
                         precision=lax.Precision.HIGHEST) + b_ref[...]


def _ada(c, w, b):
    bsz, d = c.shape
    n = w.shape[1]
    tn = 2048
    return pl.pallas_call(
        _ada_kernel,
        grid=(n // tn,),
        in_specs=[pl.BlockSpec((bsz, d), lambda j: (0, 0)),
                  pl.BlockSpec((d, tn), lambda j: (0, j)),
                  pl.BlockSpec((1, tn), lambda j: (0, j))],
        out_specs=pl.BlockSpec((bsz, tn), lambda j: (0, j)),
        out_shape=jax.ShapeDtypeStruct((bsz, n), F32),
        compiler_params=_cparams("arbitrary"),
        name="ada",
    )(c, w, b.reshape(1, n))


_C_GQ, _C_GK, _C_GV, _C_GR, _C_DK, _C_GA, _C_GB, _C_END = 0, 512, 1024, 2048, 3072, 4096, 5120, 6144
_R_DQ, _R_DV, _R_END = 0, 1024, 2048


def _inproj_kernel(x_ref, sh_ref, sc_ref, g_ref, pos_ref, post_ref, invr_ref, invc_ref,
                   wn_ref, wlr_ref, wt_ref,
                   gq_ref, gk_ref, gv_ref, gr_ref, lr_ref, dk1_ref, dk2_ref, ga_ref, gb_ref,
                   dqt_ref, dvt_ref):
    tm = x_ref.shape[1]
    h = _rms(x_ref[0], g_ref[...]) * (1.0 + sc_ref[0]) + sh_ref[0]
    hb = h.astype(BF16)

    def proj(c0, c1):
        return _dot(hb, wn_ref[:, c0:c1])

    gq_ref[0] = (proj(_C_GQ, _C_GK) * (GLA_DK ** -0.5)).astype(BF16)
    gk_ref[0] = proj(_C_GK, _C_GV).astype(BF16)
    gv_ref[0] = proj(_C_GV, _C_GR).astype(BF16)
    gr_ref[0] = proj(_C_GR, _C_DK).astype(BF16)
    ga_ref[0] = proj(_C_GA, _C_GB).astype(BF16)
    gb_ref[0] = proj(_C_GB, _C_END).astype(BF16)
    lr_ref[0] = _dot(hb, wlr_ref[...])

    ang = pos_ref[0] * invr_ref[...]
    cs = jnp.cos(ang)
    sn = jnp.sin(ang)
    lane = lax.broadcasted_iota(jnp.int32, (tm, LANES), 1)
    first = (lane % DIFF_HD) < (DIFF_HD // 2)
    sn = jnp.where(first, -sn, sn)
    y = proj(_C_DK, _C_GA)
    pad = jnp.where(lane == DIFF_HD, 1.0, 0.0)
    for hd in range(DIFF_HEADS):
        yb = y[:, hd * LANES:(hd + 1) * LANES]
        rot = jnp.where(first, pltpu.roll(yb, LANES - DIFF_HD // 2, 1), pltpu.roll(yb, DIFF_HD // 2, 1))
        kr = yb * cs + rot * sn
        dk1_ref[0, :, hd * LANES:(hd + 1) * LANES] = jnp.where(lane < DIFF_HD, kr, pad).astype(BF16)
        dk2_ref[0, :, hd * LANES:(hd + 1) * LANES] = jnp.where(lane < DIFF_HD, pltpu.roll(kr, DIFF_HD, 1),
                                                                pad).astype(BF16)

    angt = invc_ref[...] * post_ref[0]
    ct = jnp.cos(angt)
    st = jnp.sin(angt)
    half = DIFF_HD // 2
    yt = _dot_nt(wt_ref[_R_DQ:_R_DV, :], hb)
    qscale = DIFF_HD ** -0.5 * math.log2(math.e)
    for blk in range(2 * DIFF_HEADS):
        r0 = blk * DIFF_HD
        x1 = yt[r0:r0 + half]
        x2 = yt[r0 + half:r0 + DIFF_HD]
        dqt_ref[0, r0:r0 + half, :] = ((x1 * ct - x2 * st) * qscale).astype(BF16)
        dqt_ref[0, r0 + half:r0 + DIFF_HD, :] = ((x2 * ct + x1 * st) * qscale).astype(BF16)
    dvt_ref[0] = _dot_nt(wt_ref[_R_DV:_R_END, :], hb).astype(BF16)


def _inproj(x, sh, sc, g, pos, wn, wlr, wt, inv_row, inv_col, tm):
    bsz, s, d = x.shape
    posf = pos.astype(F32)
    nat = lambda n: pl.BlockSpec((1, tm, n), lambda b, i: (b, i, 0))
    tr = lambda n: pl.BlockSpec((1, n, tm), lambda b, i: (b, 0, i))
    vec = pl.BlockSpec((1, 1, d), lambda b, i: (b, 0, 0))
    shp = lambda *dims: jax.ShapeDtypeStruct(dims, BF16)
    return pl.pallas_call(
        _inproj_kernel,
        grid=(bsz, s // tm),
        in_specs=[nat(d), vec, vec, _resident((1, d)),
                  pl.BlockSpec((1, tm, 1), lambda b, i: (b, i, 0)),
                  pl.BlockSpec((1, 1, tm), lambda b, i: (b, 0, i)),
                  _resident(inv_row.shape), _resident(inv_col.shape),
                  _resident(wn.shape), _resident(wlr.shape), _resident(wt.shape)],
        out_specs=[nat(512), nat(512), nat(1024), nat(1024), nat(LANES), nat(1024), nat(1024), nat(1024), nat(1024),
                   tr(1024), tr(1024)],
        out_shape=[shp(bsz, s, 512), shp(bsz, s, 512), shp(bsz, s, 1024), shp(bsz, s, 1024),
                   jax.ShapeDtypeStruct((bsz, s, LANES), F32),
                   shp(bsz, s, 1024), shp(bsz, s, 1024), shp(bsz, s, 1024), shp(bsz, s, 1024),
                   shp(bsz, 1024, s), shp(bsz, 1024, s)],
        compiler_params=_cparams("arbitrary", "arbitrary"),
        name="inproj",
    )(x, sh, sc, g, posf.reshape(bsz, s, 1), posf.reshape(bsz, 1, s), inv_row, inv_col, wn, wlr, wt)


def _gla_direction(q_ref, k_ref, v_ref, lr_ref, wh_ref, wl_ref, ba_ref, tri_ref, o_ref, s_ref, reverse):
    blk = q_ref.shape[1]
    lr = lr_ref[0]
    lr_hi = lr.astype(BF16)
    lr_lo = (lr - lr_hi.astype(F32)).astype(BF16)
    z = _dot(lr_hi, wh_ref[...]) + _dot(lr_lo, wh_ref[...]) + _dot(lr_hi, wl_ref[...]) + ba_ref[...]
    logg = -(jnp.maximum(-z, 0.0) + jnp.log1p(jnp.exp(-jnp.abs(z)))) * (1.0 / GLA_GATE_NORM)
    lg_hi = logg.astype(BF16)
    lg_lo = (logg - lg_hi.astype(F32)).astype(BF16)
    yield
    tri = tri_ref[...]
    sub = tri.shape[0]
    q = q_ref[0].astype(F32)
    k = k_ref[0].astype(F32)
    v = v_ref[0]
    b, qe, o_intra = [], [], []
    for r0 in range(0, blk, sub):
        rows = slice(r0, r0 + sub)
        b_s = _dot(tri, lg_hi[rows]) + _dot(tri, lg_lo[rows])
        qe_s = (q[rows] * jnp.exp(b_s)).astype(BF16)
        ke_s = (k[rows] * jnp.exp(-b_s)).astype(BF16)
        att = jnp.where(tri > 0, _dot_nt(qe_s, ke_s), 0.0).astype(BF16)
        b.append(b_s)
        qe.append(qe_s)
        o_intra.append(_dot(att, v[rows]))
        yield
    b = jnp.concatenate(b, axis=0)
    qe = jnp.concatenate(qe, axis=0)
    o_intra = jnp.concatenate(o_intra, axis=0)

    state = s_ref[...]
    nchunk = blk // GLA_CHUNK
    order = range(nchunk - 1, -1, -1) if reverse else range(nchunk)
    for n in order:
        r0 = n * GLA_CHUNK
        edge = r0 if reverse else r0 + GLA_CHUNK - 1
        b_c = b[r0:r0 + GLA_CHUNK]
        b_e = b[edge:edge + 1]
        kend = (k[r0:r0 + GLA_CHUNK] * jnp.exp(b_e - b_c)).astype(BF16)
        o_inter = _dot_nt(qe[r0:r0 + GLA_CHUNK], state.astype(BF16))
        o_ref[0, r0:r0 + GLA_CHUNK, :] = (o_intra[r0:r0 + GLA_CHUNK] + o_inter).astype(BF16)
        state = state * jnp.exp(b_e) + lax.dot_general(v[r0:r0 + GLA_CHUNK], kend, (((0,), (0,)), ((), ())),
                                                         preferred_element_type=F32)
        yield
    s_ref[...] = state


def _gla_kernel(qf, kf, vf, lrf, qb, kb, vb, lrb,
                wfh, wfl, baf, wbh, wbl, bab, trif, trib,
                of_ref, ob_ref, sf_ref, sb_ref):
    @pl.when(pl.program_id(2) == 0)
    def _():
        sf_ref[...] = jnp.zeros_like(sf_ref)
        sb_ref[...] = jnp.zeros_like(sb_ref)

    live = [_gla_direction(qf, kf, vf, lrf, wfh, wfl, baf, trif, of_ref, sf_ref, reverse=False),
            _gla_direction(qb, kb, vb, lrb, wbh, wbl, bab, trib, ob_ref, sb_ref, reverse=True)]
    while live:
        live = [d for d in live if next(d, True) is None]


def _gla(gq, gk, gv, lr, wa_f, ba_f, wa_b, ba_b, blk):
    bsz, s, _ = gq.shape
    nblk = s // blk
    def pad_rows(w, r0):
        return jnp.zeros((LANES, w.shape[1]), F32).at[r0:r0 + GLA_LOWRANK].set(w)

    def hi_lo(w):
        hi = w.astype(BF16)
        return hi, (w - hi.astype(F32)).astype(BF16)

    wfh, wfl = hi_lo(pad_rows(wa_f, 0))
    wbh, wbl = hi_lo(pad_rows(wa_b, GLA_LOWRANK))
    sub = min(blk, GLA_SUB)
    r = np.arange(sub)
    same = (r[:, None] // GLA_CHUNK) == (r[None, :] // GLA_CHUNK)
    tri_f = jnp.asarray(same & (r[None, :] <= r[:, None]), BF16)
    tri_b = jnp.asarray(same & (r[None, :] >= r[:, None]), BF16)

    fw = lambda b, h, i: (b, i, h)
    bw = lambda b, h, i: (b, nblk - 1 - i, h)

    def seq_specs(tok):
        return [pl.BlockSpec((1, blk, GLA_DK), tok), pl.BlockSpec((1, blk, GLA_DK), tok),
                pl.BlockSpec((1, blk, GLA_DV), tok),
                pl.BlockSpec((1, blk, LANES), lambda b, h, i, t=tok: (t(b, h, i)[0], t(b, h, i)[1], 0))]

    wspec = pl.BlockSpec((LANES, GLA_DK), lambda b, h, i: (0, h))
    bspec = pl.BlockSpec((1, GLA_DK), lambda b, h, i: (0, h))
    out_f = pl.BlockSpec((1, blk, GLA_DV), fw)
    out_b = pl.BlockSpec((1, blk, GLA_DV), bw)
    o_shape = jax.ShapeDtypeStruct((bsz, s, GLA_HEADS * GLA_DV), BF16)
    return pl.pallas_call(
        _gla_kernel,
        grid=(bsz, GLA_HEADS, nblk),
        in_specs=seq_specs(fw) + seq_specs(bw) + [wspec, wspec, bspec, wspec, wspec, bspec,
                                                             _resident((sub, sub)), _resident((sub, sub))],
        out_specs=[out_f, out_b],
        out_shape=[o_shape, o_shape],
        scratch_shapes=[pltpu.VMEM((GLA_DV, GLA_DK), F32), pltpu.VMEM((GLA_DV, GLA_DK), F32)],
        compiler_params=_cparams("arbitrary", "arbitrary", "arbitrary"),
        name="gla",
    )(gq, gk, gv, lr, gq, gk, gv, lr,
      wfh, wfl, ba_f.reshape(1, -1), wbh, wbl, ba_b.reshape(1, -1), tri_f, tri_b)


ATTN_SAFE_SHIFT = 40.0
ATTN_NORM_CHUNK = 1024


def _ring(nk, nslot, stages):
    assert nk % nslot == 0
    offs = [off for _, off in stages]

    def static_iter(i):
        for fn, off in stages:
            if 0 <= i + off < nk:
                fn(i + off, (i + off) % nslot)

    def full(g):
        return all(0 <= g * nslot + u + off < nk for u in range(nslot) for off in offs)

    ngroup = nk // nslot
    interior = [g for g in range(ngroup) if full(g)]
    g_lo, g_hi = (interior[0], interior[-1] + 1) if interior else (ngroup, ngroup)
    for i in range(-max(offs), g_lo * nslot):
        static_iter(i)
    if g_hi - g_lo == 1:
        for i in range(g_lo * nslot, g_hi * nslot):
            static_iter(i)
    elif g_hi > g_lo:
        def group(g, carry):
            for u in range(nslot):
                for fn, off in stages:
                    fn(g * nslot + u + off, (u + off) % nslot)
            return carry
        lax.fori_loop(g_lo, g_hi, group, 0)
    for i in range(g_hi * nslot, nk - min(offs)):
        static_iter(i)


def _attn_kernel(lq1_ref, lk1_ref, lq2_ref, lk2_ref, qt_ref, k1_ref, k2_ref, vt_ref, o_ref,
                 s_ref, p_ref, a_ref, mt_ref, m_ref, l_ref, acc_ref, kmax_ref, *, tk, lambda_init):
    k_refs = (k1_ref, k2_ref)
    nkeys = k1_ref.shape[1]
    nk = nkeys // tk
    nslot = p_ref.shape[0]
    dist = nslot // 2
    nslot_slow = s_ref.shape[0]
    dist_slow = nslot_slow // 2
    tq = qt_ref.shape[2]
    nchunk = max(nkeys // ATTN_NORM_CHUNK, 1)
    kc = nkeys // nchunk

    @pl.when(pl.program_id(2) == 0)
    def _():
        lane = lax.broadcasted_iota(jnp.int32, (kc, LANES), 1)
        for mi in range(2):
            def chunk(i, mx):
                kk = k_refs[mi][0, pl.ds(pl.multiple_of(i * kc, kc), kc), :].astype(F32)
                kk = jnp.where(lane < DIFF_HD, kk, 0.0)
                n2 = jnp.sum(kk * kk, axis=1, keepdims=True)
                return jnp.maximum(mx, jnp.max(n2, axis=0, keepdims=True))
            kmax_ref[mi] = jnp.sqrt(lax.fori_loop(0, nchunk, chunk, jnp.zeros((1, 1), F32)))

    qt = qt_ref[0]
    row = lax.broadcasted_iota(jnp.int32, (16, tq), 0)
    q_shift, q_plain, bound = [], [], []
    for mi in range(2):
        qm = qt[mi * DIFF_HD:(mi + 1) * DIFF_HD]
        qf = qm.astype(F32)
        c = jnp.sqrt(jnp.sum(qf * qf, axis=0, keepdims=True)) * kmax_ref[mi]
        c = c.astype(BF16).astype(F32)
        shift_rows = jnp.where(row == 0, -c, 0.0).astype(BF16)
        q_shift.append(jnp.concatenate([qm, shift_rows, jnp.zeros((DIFF_HD - 16, tq), BF16)], axis=0))
        q_plain.append(jnp.concatenate([qm, jnp.zeros((DIFF_HD, tq), BF16)], axis=0))
        bound.append(jnp.max(c))
    safe = jnp.maximum(bound[0], bound[1]) <= ATTN_SAFE_SHIFT
    acc_ref[...] = jnp.zeros_like(acc_ref)
    l_ref[...] = jnp.zeros_like(l_ref)

    def key_tile(mi, t):
        return k_refs[mi][0, pl.ds(pl.multiple_of(t * tk, tk), tk), :]

    def value_tile(t):
        return vt_ref[0, :, pl.ds(pl.multiple_of(t * tk, tk), tk)]

    @pl.when(safe)
    def _():
        def probs(t, slot):
            for mi in range(2):
                p = jnp.exp2(_dot(key_tile(mi, t), q_shift[mi]))
                l_ref[mi] += jnp.sum(p, axis=0, keepdims=True)
                p_ref[slot, mi] = p.astype(BF16)

        def values(t, slot):
            vtj = value_tile(t)
            for mi in range(2):
                acc_ref[mi] += _dot(vtj, p_ref[slot, mi])

        _ring(nk, nslot, [(values, 0), (probs, dist)])

    @pl.when(jnp.logical_not(safe))
    def _():
        m_ref[...] = jnp.full(m_ref.shape, -jnp.inf, F32)

        def scores(t, slot):
            for mi in range(2):
                s = _dot(key_tile(mi, t), q_plain[mi])
                s_ref[slot, mi] = s
                mt_ref[slot, mi] = jnp.max(s, axis=0, keepdims=True)

        def softmax(t, slot):
            for mi in range(2):
                m_old = m_ref[mi]
                m_new = jnp.maximum(m_old, mt_ref[slot, mi])
                a_ref[slot, mi] = jnp.exp2(m_old - m_new)
                p = jnp.exp2(s_ref[slot, mi] - m_new)
                l_ref[mi] = a_ref[slot, mi] * l_ref[mi] + jnp.sum(p, axis=0, keepdims=True)
                p_ref[slot, mi] = p.astype(BF16)
                m_ref[mi] = m_new

        def values(t, slot):
            vtj = value_tile(t)
            for mi in range(2):
                acc_ref[mi] = acc_ref[mi] * a_ref[slot, mi] + _dot(vtj, p_ref[slot, mi])

        _ring(nk, nslot_slow, [(softmax, 0), (values, -dist_slow), (scores, dist_slow)])

    lam = (jnp.exp(jnp.sum(lq1_ref[...] * lk1_ref[...], axis=-1, keepdims=True))
           - jnp.exp(jnp.sum(lq2_ref[...] * lk2_ref[...], axis=-1, keepdims=True)) + lambda_init)
    ot = acc_ref[0] * (1.0 / l_ref[0]) - lam * (acc_ref[1] * (1.0 / l_ref[1]))
    o_ref[0] = ot.T.astype(BF16)


def _attn(dqt, dk1, dk2, dvt, lq1, lk1, lq2, lk2, lambda_init, tq, tk):
    bsz, s, _ = dk1.shape
    lspec = _resident((1, DIFF_HD))
    kspec = pl.BlockSpec((1, s, LANES), lambda b, h, i: (b, 0, h))
    return pl.pallas_call(
        functools.partial(_attn_kernel, tk=tk, lambda_init=lambda_init),
        grid=(bsz, DIFF_HEADS, s // tq),
        in_specs=[lspec, lspec, lspec, lspec,
                  pl.BlockSpec((1, 2 * DIFF_HD, tq), lambda b, h, i: (b, h, i)),
                  kspec, kspec,
                  pl.BlockSpec((1, DIFF_VD, s), lambda b, h, i: (b, h, 0))],
        out_specs=pl.BlockSpec((1, tq, DIFF_VD), lambda b, h, i: (b, i, h)),
        out_shape=jax.ShapeDtypeStruct((bsz, s, DIFF_HEADS * DIFF_VD), BF16),
        scratch_shapes=[pltpu.VMEM((2, 2, tk, tq), F32), pltpu.VMEM((4, 2, tk, tq), BF16),
                        pltpu.VMEM((4, 2, 1, tq), F32), pltpu.VMEM((4, 2, 1, tq), F32),
                        pltpu.VMEM((2, 1, tq), F32), pltpu.VMEM((2, 1, tq), F32),
                        pltpu.VMEM((2, DIFF_VD, tq), F32),
                        pltpu.VMEM((2, 1, 1), F32)],
        compiler_params=_cparams("arbitrary", "arbitrary", "arbitrary"),
        name="attn",
    )(lq1.reshape(1, -1), lk1.reshape(1, -1), lq2.reshape(1, -1), lk2.reshape(1, -1), dqt, dk1, dk2, dvt)


def _merge_kernel(of_ref, ob_ref, gr_ref, od_ref, ga_ref, gb_ref, x_ref, gt_ref, gg_ref, dg_ref,
                  wg_ref, wd_ref, wo_ref, o_ref, *, lambda_init):
    og = of_ref[0].astype(F32) + ob_ref[0].astype(F32)
    gg = gg_ref[...]
    og = jnp.concatenate(
        [_rms(og[:, h * GLA_DV:(h + 1) * GLA_DV], gg[:, h * GLA_DV:(h + 1) * GLA_DV]) for h in range(GLA_HEADS)],
        axis=-1)
    gr = gr_ref[0].astype(F32)
    y_gla = _dot((og * (gr * jax.nn.sigmoid(gr))).astype(BF16), wg_ref[...])
    od = od_ref[0].astype(F32)
    dg = dg_ref[...]
    od = jnp.concatenate(
        [_rms(od[:, h * DIFF_VD:(h + 1) * DIFF_VD], dg[:, h * DIFF_VD:(h + 1) * DIFF_VD]) for h in range(DIFF_HEADS)],
        axis=-1) * (1.0 - lambda_init)
    y_diff = _dot(od.astype(BF16), wd_ref[...])
    merged = (jax.nn.sigmoid(ga_ref[0].astype(F32)) * y_gla + jax.nn.sigmoid(gb_ref[0].astype(F32)) * y_diff)
    o_ref[0] = x_ref[0] + gt_ref[0] * _dot(merged.astype(BF16), wo_ref[...])


def _merge(o_f, o_b, gr, o_d, ga, gb, x, gt1, gla_g, diff_g, wg, wd, wo, lambda_init, tm):
    bsz, s, d = x.shape
    tile = pl.BlockSpec((1, tm, d), lambda b, i: (b, i, 0))
    vec = pl.BlockSpec((1, 1, d), lambda b, i: (b, 0, 0))
    return pl.pallas_call(
        functools.partial(_merge_kernel, lambda_init=lambda_init),
        grid=(bsz, s // tm),
        in_specs=[tile] * 7 + [vec, _resident((1, d)), _resident((1, d)),
                               _resident((d, d)), _resident((d, d)), _resident((d, d))],
        out_specs=tile,
        out_shape=jax.ShapeDtypeStruct((bsz, s, d), F32),
        compiler_params=_cparams("arbitrary", "arbitrary"),
        name="merge",
    )(o_f, o_b, gr, o_d, ga, gb, x, gt1, gla_g.reshape(1, d), diff_g.reshape(1, d), wg, wd, wo)


_CAND_ROW_COUNT = (16, 8, 5, 4, 3, 2, 2, 2)
_SUB = 8
_SENTINEL = 2.0 ** 100


def _top16(a):
    work = a
    vals = []
    for r in range(PEER_TOPK):
        m = jnp.max(work, axis=0, keepdims=True)
        vals.append(m)
        work = jnp.where(work == m, -_SENTINEL * (1.0 + r / 32.0), work)
    rank = jnp.where(work <= -_SENTINEL, (work * (-1.0 / _SENTINEL) - 1.0) * 32.0, float(PEER_TOPK))
    return jnp.concatenate(vals, axis=0), rank


def _oddeven_merge_sort_pairs(n):
    pairs = []
    p = 1
    while p < n:
        k = p
        while k >= 1:
            for j in range(k % p, n - k, 2 * k):
                for i in range(min(k, n - j - k)):
                    if (i + j) // (2 * p) == (i + j + k) // (2 * p):
                        pairs.append((i + j, i + j + k))
            k //= 2
        p *= 2
    return pairs


def _top16_values(a):
    nslab = a.shape[0] // _SUB
    assert nslab == PEER_TOPK
    slabs = [a[k * _SUB:(k + 1) * _SUB] for k in range(nslab)]

    def order(i, j):
        slabs[i], slabs[j] = jnp.maximum(slabs[i], slabs[j]), jnp.minimum(slabs[i], slabs[j])

    for i, j in _oddeven_merge_sort_pairs(nslab):
        order(i, j)
    shift = _SUB // 2
    while shift >= 1:
        other = [pltpu.roll(x, shift, 0) for x in slabs]
        slabs = [jnp.maximum(slabs[k], other[nslab - 1 - k]) for k in range(nslab)]
        d = nslab // 2
        while d >= 1:
            for k in range(nslab):
                if k & d == 0:
                    order(k, k + d)
            d //= 2
        shift //= 2
    return jnp.concatenate([x[0:1] for x in slabs], axis=0)


def _route_head(a, b):
    t = a.shape[1]
    top_a = _top16_values(a)
    top_b, rank_b = _top16(b)
    rowi = lax.broadcasted_iota(jnp.int32, (_SUB, t), 0)
    pieces = [top_b + top_a[0:1]]
    for r in range(1, _SUB):
        piece = top_b[0:_SUB] + top_a[r:r + 1]
        n = _CAND_ROW_COUNT[r]
        pieces.append(piece if n == _SUB else jnp.where(rowi < n, piece, -jnp.inf))
    pieces.append(top_a[_SUB:PEER_TOPK] + top_b[0:1])
    cand = jnp.concatenate(pieces, axis=0)
    work = cand
    thr = None
    for _ in range(PEER_TOPK):
        thr = jnp.max(work, axis=0, keepdims=True)
        work = jnp.where(work == thr, -jnp.inf, work)
    sel = cand >= thr
    smax = top_a[0:1] + top_b[0:1]
    z = jnp.sum(jnp.where(sel, jnp.exp(cand - smax), 0.0), axis=0, keepdims=True)
    self32 = sel.astype(F32)
    counts = [jnp.sum(self32[0:PEER_TOPK], axis=0, keepdims=True)]
    for r in range(1, _SUB):
        r0 = PEER_TOPK + (r - 1) * _SUB
        counts.append(jnp.sum(self32[r0:r0 + _SUB], axis=0, keepdims=True))
    tail0 = PEER_TOPK + (_SUB - 1) * _SUB
    for r in range(_SUB, PEER_TOPK):
        counts.append(self32[tail0 + r - _SUB:tail0 + r - _SUB + 1])
    cnt = jnp.zeros(a.shape, F32)
    for r in range(_SUB):
        cnt = jnp.where(a == top_a[r:r + 1], counts[r], cnt)
    tail = (a < top_a[_SUB - 1:_SUB]) & (a >= top_a[PEER_TOPK - 1:PEER_TOPK]) & (a + top_b[0:1] >= thr)
    cnt = jnp.where(tail, 1.0, cnt)
    wa = jnp.exp(a - top_a[0:1]) * (0.5 / z)
    wb = jnp.exp(b - top_b[0:1])
    return rank_b, wb, cnt, wa


def _route_kernel(x_ref, sh_ref, sc_ref, g_ref, wqt_ref, sk_ref, h2t_ref, rb_ref, wb_ref, cnt_ref, wa_ref):
    h2 = _rms(x_ref[0], g_ref[...]) * (1.0 + sc_ref[0]) + sh_ref[0]
    hb = h2.astype(BF16)
    h2t_ref[0] = h2.T.astype(BF16)
    qt = _dot_nt(wqt_ref[...], hb).astype(BF16)
    for h in range(PEER_HEADS):
        g0 = 2 * h * PEER_NKEYS
        a = _dot(sk_ref[2 * h], qt[g0:g0 + PEER_NKEYS])
        b = _dot(sk_ref[2 * h + 1], qt[g0 + PEER_NKEYS:g0 + 2 * PEER_NKEYS])
        rank_b, wb, cnt, wa = _route_head(a, b)
        rb_ref[0, h] = rank_b.astype(BF16)
        wb_ref[0, h] = wb.astype(BF16)
        cnt_ref[0, h] = cnt
        wa_ref[0, h] = wa


def _route(x1, sh2, sc2, g, wqt, sk, tm):
    bsz, s, d = x1.shape
    vec = pl.BlockSpec((1, 1, d), lambda b, i: (b, 0, 0))
    gate = pl.BlockSpec((1, PEER_HEADS, PEER_NKEYS, tm), lambda b, i: (b, 0, 0, i))
    gshape = lambda dt: jax.ShapeDtypeStruct((bsz, PEER_HEADS, PEER_NKEYS, s), dt)
    return pl.pallas_call(
        _route_kernel,
        grid=(bsz, s // tm),
        in_specs=[pl.BlockSpec((1, tm, d), lambda b, i: (b, i, 0)), vec, vec, _resident((1, d)),
                  _resident(wqt.shape), _resident(sk.shape)],
        out_specs=[pl.BlockSpec((1, d, tm), lambda b, i: (b, 0, i)), gate, gate, gate, gate],
        out_shape=[jax.ShapeDtypeStruct((bsz, d, s), BF16), gshape(BF16), gshape(BF16), gshape(F32), gshape(F32)],
        compiler_params=_cparams("arbitrary", "arbitrary"),
        name="route",
    )(x1, sh2, sc2, g, wqt, sk)


DENSE_ROWS = PEER_NKEYS


def _dense_kernel(h2t_ref, u_ref, vt_ref, rb_ref, wb_ref, cnt_ref, wa_ref, x_ref, gt_ref, fg_ref, fsh_ref, fsc_ref,
                  o_ref, acc_ref, p_ref, *, eblk):
    e = pl.program_id(2)

    @pl.when(e == 0)
    def _():
        acc_ref[...] = jnp.zeros_like(acc_ref)

    half = h2t_ref.shape[2] // 2
    nchunk = eblk // DENSE_ROWS
    per = DENSE_ROWS // PEER_NKEYS
    dq = acc_ref.shape[0] // nchunk

    def front(hf, c):
        cols = slice(hf * half, (hf + 1) * half)
        rows = slice(c * DENSE_ROWS, (c + 1) * DENSE_ROWS)
        gates = []
        for ii in range(c * per, (c + 1) * per):
            w = None
            for h in range(PEER_HEADS):
                cnt_i = cnt_ref[0, h, 0, ii:ii + 1, cols].astype(BF16)
                wa_i = wa_ref[0, h, 0, ii:ii + 1, cols].astype(BF16)
                wh = jnp.where(rb_ref[0, h, :, cols] < cnt_i, wb_ref[0, h, :, cols], jnp.zeros((), BF16)) * wa_i
                w = wh if w is None else w + wh
            gates.append(w)
        gate = jnp.concatenate(gates, axis=0)
        at = _dot(u_ref[rows, :], h2t_ref[0, :, cols])
        act = at * (1.0 + lax.erf(at * math.sqrt(0.5)))
        p_ref[rows, cols] = act.astype(BF16) * gate

    def back(hf, c):
        cols = slice(hf * half, (hf + 1) * half)
        out = slice(c * dq, (c + 1) * dq)
        acc_ref[out, cols] += _dot(vt_ref[0, out, :], p_ref[:, cols])

    for c in range(nchunk):
        front(0, c)
    for c in range(nchunk):
        front(1, c)
        back(0, c)
    for c in range(nchunk):
        back(1, c)

    @pl.when(e == pl.num_programs(2) - 1)
    def _():
        y = x_ref[0] + gt_ref[0] * acc_ref[...].T
        o_ref[0] = _rms(y, fg_ref[...]) * (1.0 + fsc_ref[0]) + fsh_ref[0]


def _dense(h2t, u, vt, rb, wb, cnt, wa, x1, gt2, fg, fsh, fsc, tm, eblk):
    bsz, s, d = x1.shape
    nblk = u.shape[0] // eblk
    gate = pl.BlockSpec((1, PEER_HEADS, PEER_NKEYS, tm), lambda b, i, e: (b, 0, 0, i))
    nrow = eblk // PEER_NKEYS
    rows = pl.BlockSpec((1, PEER_HEADS, 1, nrow, tm), lambda b, i, e: (b, 0, e, 0, i))
    by_block = lambda a: a.reshape(bsz, PEER_HEADS, nblk, nrow, s)
    tile = pl.BlockSpec((1, tm, d), lambda b, i, e: (b, i, 0))
    vec = pl.BlockSpec((1, 1, d), lambda b, i, e: (b, 0, 0))
    return pl.pallas_call(
        functools.partial(_dense_kernel, eblk=eblk),
        grid=(bsz, s // tm, nblk),
        in_specs=[pl.BlockSpec((1, d, tm), lambda b, i, e: (b, 0, i)),
                  pl.BlockSpec((eblk, d), lambda b, i, e: (e, 0)),
                  pl.BlockSpec((1, d, eblk), lambda b, i, e: (e, 0, 0)),
                  gate, gate, rows, rows, tile, vec, _resident((1, d)), vec, vec],
        out_specs=tile,
        out_shape=jax.ShapeDtypeStruct((bsz, s, d), F32),
        scratch_shapes=[pltpu.VMEM((d, tm), F32), pltpu.VMEM((eblk, tm), BF16)],
        compiler_params=_cparams("arbitrary", "arbitrary", "arbitrary"),
        name="dense",
    )(h2t, u, vt, rb, wb, by_block(cnt), by_block(wa), x1, gt2, fg.reshape(1, d), fsh, fsc)


TOKEN_TILE = 512
GLA_BLOCK = 1024
ATTN_Q_TILE = 2048
ATTN_K_TILE = 256
ROUTE_TILE = 256
DENSE_EXPERTS = 8 * PEER_NKEYS


def _tile(s, want):
    return want if s % want == 0 else s


def kernel(x, c, positions, w_ada, b_ada, norm1_g, w_in, gla_wa_fw, gla_ba_fw, gla_wa_bw, gla_ba_bw, gla_norm_g,
           diff_lq1, diff_lk1, diff_lq2, diff_lk2, diff_norm_g, w_gla_proj, w_diff_proj, w_out, norm2_g, peer_wq,
           peer_subkeys, peer_u, peer_v, w_final_ada, b_final_ada, normf_g):
    bsz, s, d = x.shape
    depth = w_ada.shape[0]
    tm = _tile(s, TOKEN_TILE)

    half = DIFF_HD // 2
    inv = (ROPE_THETA ** (-np.arange(0, DIFF_HD, 2, dtype=np.float32) / DIFF_HD)).astype(np.float32)
    inv_row = jnp.asarray(np.tile(inv, LANES // half).reshape(1, LANES))
    inv_col = jnp.asarray(inv.reshape(half, 1))

    fmod = _ada(c, w_final_ada, b_final_ada)
    f_shift = fmod[:, :d].reshape(bsz, 1, d)
    f_scale = fmod[:, d:].reshape(bsz, 1, d)

    qk, gv_w, lr_w = 2 * GLA_HEADS * GLA_DK, GLA_HEADS * GLA_DV, 2 * GLA_LOWRANK
    o_gq, o_gk, o_gv, o_gr = 0, qk // 2, qk, qk + gv_w
    o_lr = o_gr + gv_w
    o_dq = o_lr + lr_w
    o_dk, o_dv, o_ga, o_gb = o_dq + d, o_dq + 2 * d, o_dq + 3 * d, o_dq + 4 * d

    for l in range(depth):
        lambda_init = 0.8 - 0.6 * math.exp(-0.3 * l)
        mod = _ada(c, w_ada[l], b_ada[l]).reshape(bsz, 1, N_ADA * d)
        sh1, sc1, gt1, sh2, sc2, gt2 = (mod[:, :, i * d:(i + 1) * d] for i in range(N_ADA))

        w = w_in[l]
        col = lambda c0, n: w[:, c0:c0 + n]
        wn = jnp.concatenate([col(o_gq, qk // 2), col(o_gk, qk // 2), col(o_gv, gv_w), col(o_gr, gv_w),
                              col(o_dk, d), col(o_ga, d), col(o_gb, d)], axis=1).astype(BF16)
        wlr = jnp.zeros((d, LANES), F32).at[:, :lr_w].set(col(o_lr, lr_w)).astype(BF16)
        wt = jnp.concatenate([col(o_dq, d), col(o_dv, d)], axis=1).T.astype(BF16)
        (gq, gk, gv, gr, lr, dk1, dk2, ga, gb, dqt, dvt) = _inproj(
            x, sh1, sc1, norm1_g[l].reshape(1, d), positions, wn, wlr, wt, inv_row, inv_col, tm)

        o_f, o_b = _gla(gq, gk, gv, lr, gla_wa_fw[l], gla_ba_fw[l], gla_wa_bw[l], gla_ba_bw[l], _tile(s, GLA_BLOCK))
        o_d = _attn(dqt, dk1, dk2, dvt, diff_lq1[l], diff_lk1[l], diff_lq2[l], diff_lk2[l], lambda_init,
                    _tile(s, ATTN_Q_TILE), _tile(s, ATTN_K_TILE))
        x1 = _merge(o_f, o_b, gr, o_d, ga, gb, x, gt1, gla_norm_g[l], diff_norm_g[l],
                    w_gla_proj[l].astype(BF16), w_diff_proj[l].astype(BF16), w_out[l].astype(BF16), lambda_init, tm)

        sk = peer_subkeys[l].reshape(2 * PEER_HEADS, PEER_NKEYS, -1).astype(BF16)
        h2t, rb, wb, cnt, wa = _route(x1, sh2, sc2, norm2_g[l].reshape(1, d), peer_wq[l].T.astype(BF16), sk,
                                      _tile(s, ROUTE_TILE))
        last = l == depth - 1
        assert last, "multi-layer stacks need the final norm split out of the expert kernel"
        eblk = DENSE_EXPERTS
        vt = peer_v[l].astype(BF16).reshape(-1, eblk, d).transpose(0, 2, 1)
        x = _dense(h2t, peer_u[l].astype(BF16), vt, rb, wb, cnt, wa, x1, gt2,
                   normf_g, f_shift, f_scale, tm, eblk)
    return x
```
